```python
import jax, jax.numpy as jnp
from jax import lax
import numpy as np

D_MODEL = 1024
BATCH = 4
SEQ = 8192
DEPTH = 2

D_FF = 2816
HG_DK = 128
HG_HEADS = D_MODEL // HG_DK
HG_DV = D_MODEL // HG_HEADS
HG_WK = HG_HEADS * HG_DK
HG_WV = HG_HEADS * HG_DV
HG_CHUNK = 64
ATT_PATTERNS = ((128, 1), (512, 4), (2048, 16))
ATT_GROUPS = 3
ATT_HEADS = 4
ATT_DH = 128
ATT_W = ATT_GROUPS * ATT_HEADS * ATT_DH
ATT_OUT = ATT_HEADS * ATT_DH
ROPE_THETA = 10000.0
EPS = 1e-6
SPLIT_SIZES = (HG_WK, HG_WK, HG_WV, HG_WV, ATT_W, ATT_W, ATT_W, D_MODEL, D_MODEL)
P_IN = sum(SPLIT_SIZES)

kernel_name = "hybrid_hgrn2_dilated_attn_macaron"


def rms(x):
    xf = x.astype(jnp.float32)
    return xf * lax.rsqrt(jnp.mean(xf * xf, axis=-1, keepdims=True) + EPS)


def rmsnorm(x, g):
    return (rms(x) * g.astype(jnp.float32)).astype(x.dtype)


def swiglu(h, w_in, w_out):
    a, b = jnp.split(h @ w_in, 2, axis=-1)
    return (jax.nn.silu(a) * b) @ w_out


def rope_tables(t):
    pos = jnp.arange(t, dtype=jnp.float32)
    inv = ROPE_THETA ** (-jnp.arange(0, ATT_DH, 2, dtype=jnp.float32) / ATT_DH)
    ang = pos[:, None] * inv[None, :]
    ang = jnp.concatenate([ang, ang], axis=-1)
    return jnp.cos(ang), jnp.sin(ang)


def apply_rope(x, cos, sin):
    x1, x2 = jnp.split(x, 2, axis=-1)
    return x * cos + jnp.concatenate([-x2, x1], axis=-1) * sin


def hgrn2_chunk_scan(q, k, v, log_f):
    b, t, h, dk = q.shape
    dv = v.shape[-1]
    n = t // HG_CHUNK

    def chunks(a):
        return a.reshape(b, n, HG_CHUNK, h, a.shape[-1]).transpose(1, 0, 3, 2, 4)

    causal = jnp.tril(jnp.ones((HG_CHUNK, HG_CHUNK), dtype=bool))[:, :, None]

    def step(state, inp):
        qc, kc, vc, gc = inp
        gcum = jnp.cumsum(gc, axis=2)
        diff = gcum[:, :, :, None, :] - gcum[:, :, None, :, :]
        decay = jnp.exp(jnp.where(causal, diff, -jnp.inf))
        attn = jnp.einsum('bhtk,bhsk,bhtsk->bhts', qc, kc, decay)
        o = jnp.einsum('bhts,bhsv->bhtv', attn, vc) + jnp.einsum(
            'bhtk,bhkv->bhtv', qc * jnp.exp(gcum), state)
        g_last = gcum[:, :, -1:, :]
        state = jnp.exp(g_last[:, :, 0, :, None]) * state + jnp.einsum(
            'bhsk,bhsv->bhkv', kc * jnp.exp(g_last - gcum), vc)
        return state, o

    s0 = jnp.zeros((b, h, dk, dv), jnp.float32)
    _, o = lax.scan(step, s0, (chunks(q), chunks(k), chunks(v), chunks(log_f)))
    return o.transpose(1, 0, 3, 2, 4).reshape(b, t, h * dv)


def dilated_window_attention(q, k, v, window, dilation):
    b, h, t, dh = q.shape
    back = window // dilation
    blk = back
    L = t // dilation
    nb = -(-L // blk)
    Lp = nb * blk

    def to_res(a):
        a = a.reshape(b, h, L, dilation, dh).transpose(0, 1, 3, 2, 4)
        return jnp.pad(a, ((0, 0), (0, 0), (0, 0), (0, Lp - L), (0, 0)))

    def kv_blocks(a):
        a = jnp.pad(a, ((0, 0), (0, 0), (0, 0), (blk, 0), (0, 0)))
        a = a.reshape(b, h, dilation, nb + 1, blk, dh)
        return jnp.concatenate([a[:, :, :, :-1], a[:, :, :, 1:]], axis=4)

    qb = to_res(q).reshape(b, h, dilation, nb, blk, dh)
    kb = kv_blocks(to_res(k))
    vb = kv_blocks(to_res(v))
    s = jnp.einsum('bhrnqd,bhrnkd->bhrnqk', qb, kb) * (dh ** -0.5)
    qi = jnp.arange(blk)[:, None]
    ki = jnp.arange(2 * blk)[None, :]
    band = (ki >= qi) & (ki <= qi + back)
    valid = (ki >= blk)[None] | (jnp.arange(nb)[:, None, None] > 0)
    mask = band[None] & valid
    s = jnp.where(mask, s, -jnp.inf)
    lse = jax.nn.logsumexp(s, axis=-1)
    p = jnp.exp(s - lse[..., None])
    o = jnp.einsum('bhrnqk,bhrnkd->bhrnqd', p, vb)
    o = o.reshape(b, h, dilation, Lp, dh)[:, :, :, :L].transpose(0, 1, 3, 2, 4)
    lse = lse.reshape(b, h, dilation, Lp)[:, :, :, :L].transpose(0, 1, 3, 2)
    return o.reshape(b, h, t, dh), lse.reshape(b, h, t)


def setup_inputs(seed: int = 0) -> dict:
    key = jax.random.key(seed)
    ks = jax.random.split(key, 16)
    f32 = jnp.float32

    def nrm(k, shape, fan_in):
        return jax.random.normal(k, shape, f32) * (fan_in ** -0.5)

    def gain(k, shape):
        return 1.0 + 0.05 * jax.random.normal(k, shape, f32)

    return {
        "x": jax.random.normal(ks[0], (BATCH, SEQ, D_MODEL), f32),
        "ffn1_norm": gain(ks[1], (DEPTH, D_MODEL)),
        "ffn1_w_in": nrm(ks[2], (DEPTH, D_MODEL, 2 * D_FF), D_MODEL),
        "ffn1_w_out": nrm(ks[3], (DEPTH, D_FF, D_MODEL), D_FF),
        "mix_norm": gain(ks[4], (DEPTH, D_MODEL)),
        "w_in": nrm(ks[5], (DEPTH, D_MODEL, P_IN), D_MODEL),
        "hgrn_lb_logits": 0.5 * jax.random.normal(ks[6], (DEPTH, HG_WK), f32),
        "hgrn_out_norm": gain(ks[7], (DEPTH, HG_WV)),
        "attn_q_norm": gain(ks[8], (DEPTH, ATT_GROUPS, ATT_DH)),
        "attn_k_norm": gain(ks[9], (DEPTH, ATT_GROUPS, ATT_DH)),
        "w_branch_a": nrm(ks[10], (DEPTH, HG_WV, D_MODEL), HG_WV),
        "w_branch_b": nrm(ks[11], (DEPTH, ATT_OUT, D_MODEL), ATT_OUT),
        "w_out": nrm(ks[12], (DEPTH, D_MODEL, D_MODEL), D_MODEL),
        "ffn2_norm": gain(ks[13], (DEPTH, D_MODEL)),
        "ffn2_w_in": nrm(ks[14], (DEPTH, D_MODEL, 2 * D_FF), D_MODEL),
        "ffn2_w_out": nrm(ks[15], (DEPTH, D_FF, D_MODEL), D_FF),
    }


def reference(x, ffn1_norm, ffn1_w_in, ffn1_w_out, mix_norm, w_in, hgrn_lb_logits,
              hgrn_out_norm, attn_q_norm, attn_k_norm, w_branch_a, w_branch_b, w_out,
              ffn2_norm, ffn2_w_in, ffn2_w_out):
    b, t, _ = x.shape
    f32 = jnp.float32
    cos, sin = rope_tables(t)
    lb_all = jnp.cumsum(jax.nn.softmax(hgrn_lb_logits.astype(f32), axis=0), axis=0)
    lb_all = lb_all - lb_all[0:1]
    split_idx = [int(s) for s in np.cumsum(SPLIT_SIZES)[:-1]]

    for l in range(DEPTH):
        x = x + 0.5 * swiglu(rmsnorm(x, ffn1_norm[l]), ffn1_w_in[l], ffn1_w_out[l])

        h = rmsnorm(x, mix_norm[l])
        hq, hf, hi, hg, aq, ak, av, ga, gb = jnp.split(h @ w_in[l], split_idx, axis=-1)

        lb = lb_all[l]
        f = lb + (1.0 - lb) * jax.nn.sigmoid(hf.astype(f32))
        q_a = jax.nn.silu(hq.astype(f32)).reshape(b, t, HG_HEADS, HG_DK)
        k_a = (1.0 - f).reshape(b, t, HG_HEADS, HG_DK)
        v_a = hi.astype(f32).reshape(b, t, HG_HEADS, HG_DV)
        log_f = jnp.log(f).reshape(b, t, HG_HEADS, HG_DK)
        o_a = hgrn2_chunk_scan(q_a, k_a, v_a, log_f)
        o_a = rms(o_a.reshape(b, t, HG_HEADS, HG_DV)).reshape(b, t, HG_WV)
        o_a = o_a * hgrn_out_norm[l].astype(f32) * jax.nn.silu(hg.astype(f32))
        y_a = o_a.astype(x.dtype) @ w_branch_a[l]

        def heads(a):
            return a.reshape(b, t, ATT_GROUPS, ATT_HEADS, ATT_DH).transpose(2, 0, 3, 1, 4).astype(f32)

        qn = attn_q_norm[l].astype(f32)[:, None, None, None, :]
        kn = attn_k_norm[l].astype(f32)[:, None, None, None, :]
        q_b = apply_rope(rms(heads(aq)) * qn, cos, sin)
        k_b = apply_rope(rms(heads(ak)) * kn, cos, sin)
        v_b = heads(av)
        outs, lses = [], []
        for g, (window, dilation) in enumerate(ATT_PATTERNS):
            o_g, lse_g = dilated_window_attention(q_b[g], k_b[g], v_b[g], window, dilation)
            outs.append(o_g)
            lses.append(lse_g)
        alpha = jax.nn.softmax(jnp.stack(lses, axis=0), axis=0)
        o_b = jnp.einsum('gbht,gbhtd->bthd', alpha, jnp.stack(outs, axis=0)).reshape(b, t, ATT_OUT)
        y_b = o_b.astype(x.dtype) @ w_branch_b[l]

        merged = jax.nn.sigmoid(ga) * y_a + jax.nn.sigmoid(gb) * y_b
        x = x + merged @ w_out[l]

        x = x + 0.5 * swiglu(rmsnorm(x, ffn2_norm[l]), ffn2_w_in[l], ffn2_w_out[l])
    return x
```

```python
import functools

import jax
import jax.numpy as jnp
import numpy as np
from jax import lax
from jax.experimental import pallas as pl
from jax.experimental.pallas import tpu as pltpu

F32 = jnp.float32
BF16 = jnp.bfloat16

D_MODEL = 1024
DEPTH = 2
D_FF = 2816
HEAD_DIM = 128
HG_HEADS = D_MODEL // HEAD_DIM
HG_W = HG_HEADS * HEAD_DIM
ATT_DILATIONS = (1, 4, 16)
ATT_BACK = 128
ATT_GROUPS = 3
ATT_HEADS = 4
ATT_GW = ATT_HEADS * HEAD_DIM
ATT_W = ATT_GROUPS * ATT_GW
ROPE_THETA = 10000.0
EPS = 1e-6

COL_HQ, COL_HF, COL_HI, COL_HG = 0, HG_W, 2 * HG_W, 3 * HG_W
COL_AQ = 4 * HG_W
COL_AK = COL_AQ + ATT_W
COL_AV = COL_AK + ATT_W
COL_GA = COL_AV + ATT_W
COL_GB = COL_GA + D_MODEL
P_IN = COL_GB + D_MODEL

VMEM_LIMIT_BYTES_V7X = 56 * 1024 * 1024

FFN_TM = 512
FFN_FC = 256
PROJ_TM = 512
PROJ_NC = 512
HG_TB = 512
HG_C = 64
ATT_TT = ATT_BACK * ATT_DILATIONS[-1]
MERGE_TM = 512


def _resident(shape):
    nd = len(shape)
    return pl.BlockSpec(shape, lambda *_: (0,) * nd, pipeline_mode=pl.Buffered(1))


def _rms_rows(x):
    return x * lax.rsqrt(jnp.mean(x * x, axis=-1, keepdims=True) + EPS)


def _silu(x):
    return x * jax.nn.sigmoid(x)


def _ffn_kernel(x_ref, g_ref, win_ref, wout_ref, o_ref, act_ref):
    x = x_ref[...]
    h = (_rms_rows(x) * g_ref[...]).astype(BF16)
    for c in range(D_FF // FFN_FC):
        lo = c * FFN_FC
        a = jnp.dot(h, win_ref[:, lo:lo + FFN_FC], preferred_element_type=F32)
        b = jnp.dot(h, win_ref[:, D_FF + lo:D_FF + lo + FFN_FC], preferred_element_type=F32)
        act_ref[:, lo:lo + FFN_FC] = (_silu(a) * b).astype(BF16)
    y = jnp.dot(act_ref[...], wout_ref[...], preferred_element_type=F32)
    o_ref[...] = x + 0.5 * y


def _ffn(x2d, gain, w_in, w_out):
    n = x2d.shape[0]
    return pl.pallas_call(
        _ffn_kernel,
        grid=(n // FFN_TM,),
        in_specs=[
            pl.BlockSpec((FFN_TM, D_MODEL), lambda i: (i, 0)),
            _resident((1, D_MODEL)),
            _resident((D_MODEL, 2 * D_FF)),
            _resident((D_FF, D_MODEL)),
        ],
        out_specs=pl.BlockSpec((FFN_TM, D_MODEL), lambda i: (i, 0)),
        out_shape=jax.ShapeDtypeStruct((n, D_MODEL), F32),
        scratch_shapes=[pltpu.VMEM((FFN_TM, D_FF), BF16)],
        compiler_params=pltpu.CompilerParams(
            dimension_semantics=("parallel",), vmem_limit_bytes=VMEM_LIMIT_BYTES_V7X),
        name="ffn",
    )(x2d, gain.reshape(1, D_MODEL), w_in, w_out)


def _proj_a_kernel(x_ref, g_ref, w_ref, qa_ref, hf_ref, va_ref, gate_ref, sga_ref, sgb_ref):
    h = (_rms_rows(x_ref[...]) * g_ref[...]).astype(BF16)
    outs = (
        (qa_ref, _silu), (hf_ref, None), (va_ref, None), (gate_ref, _silu),
        (sga_ref, jax.nn.sigmoid), (sgb_ref, jax.nn.sigmoid))
    for s, (o_ref, fn) in enumerate(outs):
        for c in range(HG_W // PROJ_NC):
            lo = c * PROJ_NC
            y = jnp.dot(h, w_ref[:, s * HG_W + lo:s * HG_W + lo + PROJ_NC], preferred_element_type=F32)
            if fn is not None:
                y = fn(y)
            o_ref[:, lo:lo + PROJ_NC] = y.astype(o_ref.dtype)


def _proj_a(x2d, gain, w_a):
    n = x2d.shape[0]
    tile = lambda: pl.BlockSpec((PROJ_TM, HG_W), lambda i: (i, 0))
    bf = jax.ShapeDtypeStruct((n, HG_W), BF16)
    return pl.pallas_call(
        _proj_a_kernel,
        grid=(n // PROJ_TM,),
        in_specs=[tile(), _resident((1, D_MODEL)), _resident((D_MODEL, 6 * HG_W))],
        out_specs=[tile() for _ in range(6)],
        out_shape=[bf, jax.ShapeDtypeStruct((n, HG_W), F32), bf, bf, bf, bf],
        compiler_params=pltpu.CompilerParams(
            dimension_semantics=("parallel",), vmem_limit_bytes=VMEM_LIMIT_BYTES_V7X),
        name="proj_a",
    )(x2d, gain.reshape(1, D_MODEL), w_a)


def _rope(y, cos, sin_signed):
    return y * cos + pltpu.roll(y, HEAD_DIM // 2, axis=1) * sin_signed


def _proj_b_kernel(x_ref, g_ref, w_ref, qn_ref, kn_ref,
                   cos0_ref, sin0_ref, cos1_ref, sin1_ref, cos2_ref, sin2_ref,
                   q0_ref, k0_ref, v0_ref, q1_ref, k1_ref, v1_ref, q2_ref, k2_ref, v2_ref,
                   slab_ref):
    h = (_rms_rows(x_ref[...]) * g_ref[...]).astype(BF16)
    tabs = ((cos0_ref, sin0_ref), (cos1_ref, sin1_ref), (cos2_ref, sin2_ref))
    outs = ((q0_ref, k0_ref, v0_ref), (q1_ref, k1_ref, v1_ref), (q2_ref, k2_ref, v2_ref))
    scale = HEAD_DIM ** -0.5
    for g, d in enumerate(ATT_DILATIONS):
        cos_ref, sin_ref = tabs[g]
        rows = PROJ_TM // d
        for s in range(3):
            o_ref = outs[g][s]
            col = s * ATT_W + g * ATT_GW
            y = jnp.dot(h, w_ref[:, col:col + ATT_GW], preferred_element_type=F32)
            for hd in range(ATT_HEADS):
                slab_ref[hd] = y[:, hd * HEAD_DIM:(hd + 1) * HEAD_DIM]
            for r in range(d):
                for hd in range(ATT_HEADS):
                    yr = slab_ref[hd, pl.ds(r, rows, stride=d), :] if d > 1 else slab_ref[hd]
                    if s < 2:
                        gain = (qn_ref if s == 0 else kn_ref)[g:g + 1, :]
                        yr = _rope(_rms_rows(yr) * gain, cos_ref[r], sin_ref[r])
                        if s == 0:
                            yr = yr * scale
                    o_ref[0, r, :, hd * HEAD_DIM:(hd + 1) * HEAD_DIM] = yr.astype(BF16)


def _proj_b(x3d, gain, w_b, qn, kn, tables):
    b, t, _ = x3d.shape
    nt = t // PROJ_TM
    in_specs = [
        pl.BlockSpec((None, PROJ_TM, D_MODEL), lambda bi, i: (bi, i, 0)),
        _resident((1, D_MODEL)),
        _resident((D_MODEL, 3 * ATT_W)),
        _resident((ATT_GROUPS, HEAD_DIM)),
        _resident((ATT_GROUPS, HEAD_DIM)),
    ]
    out_specs, out_shape = [], []
    for d in ATT_DILATIONS:
        for _ in range(2):
            in_specs.append(pl.BlockSpec((d, PROJ_TM // d, HEAD_DIM), lambda bi, i: (0, i, 0)))
    for d in ATT_DILATIONS:
        for _ in range(3):
            out_specs.append(pl.BlockSpec((1, d, PROJ_TM // d, ATT_GW), lambda bi, i: (bi, 0, i, 0)))
            out_shape.append(jax.ShapeDtypeStruct((b, d, t // d, ATT_GW), BF16))
    return pl.pallas_call(
        _proj_b_kernel,
        grid=(b, nt),
        in_specs=in_specs,
        out_specs=out_specs,
        out_shape=out_shape,
        scratch_shapes=[pltpu.VMEM((ATT_HEADS, PROJ_TM, HEAD_DIM), F32)],
        compiler_params=pltpu.CompilerParams(
            dimension_semantics=("parallel", "parallel"), vmem_limit_bytes=VMEM_LIMIT_BYTES_V7X),
        name="proj_b",
    )(x3d, gain.reshape(1, D_MODEL), w_b, qn, kn, *tables)


def _rope_tables(t):
    pos = jnp.arange(t, dtype=F32)
    inv = ROPE_THETA ** (-jnp.arange(0, HEAD_DIM, 2, dtype=F32) / HEAD_DIM)
    ang = pos[:, None] * inv[None, :]
    cos = jnp.concatenate([jnp.cos(ang), jnp.cos(ang)], axis=-1)
    sin = jnp.concatenate([-jnp.sin(ang), jnp.sin(ang)], axis=-1)
    tables = []
    for d in ATT_DILATIONS:
        for tab in (cos, sin):
            tables.append(tab.reshape(t // d, d, HEAD_DIM).transpose(1, 0, 2))
    return tables


def _hgrn_levels(c):
    levels = []
    m = 1
    while m < c:
        levels.append(m)
        m *= 2
    return levels


def _hgrn_kernel(layer, qa_ref, hf_ref, va_ref, gate_ref, lbl_ref, on_ref, o_ref, state_ref):
    c = HG_C

    @pl.when(pl.program_id(1) == 0)
    def _():
        state_ref[...] = jnp.zeros_like(state_ref)

    logits = lbl_ref[...]
    e = jnp.exp(logits - jnp.max(logits, axis=0, keepdims=True))
    sm = e / jnp.sum(e, axis=0, keepdims=True)
    lb_all = jnp.sum(sm[1:layer + 1], axis=0, keepdims=True) if layer > 0 else jnp.zeros((1, HG_W), F32)

    row = lax.broadcasted_iota(jnp.int32, (c, c), 0)
    colm = lax.broadcasted_iota(jnp.int32, (c, c), 1)
    tri = (colm <= row).astype(BF16)
    rix = lax.broadcasted_iota(jnp.int32, (c, HEAD_DIM), 0)
    sub = lax.broadcasted_iota(jnp.int32, (c // 8, 8, HEAD_DIM), 1)
    levels = _hgrn_levels(c)
    diag = row == colm
    sgns = [jnp.where(rix % (2 * m) >= m, 1.0, -1.0).astype(F32) for m in levels]
    sels = [((row // (2 * m)) == (colm // (2 * m))) & ((row % (2 * m)) >= m) & ((colm % (2 * m)) < m)
            for m in levels]

    def chunk_body(ci, carry):
        r0 = pl.multiple_of(ci * c, c)
        for hd in range(HG_HEADS):
            cs = slice(hd * HEAD_DIM, (hd + 1) * HEAD_DIM)
            lb = lb_all[:, cs]
            q = qa_ref[pl.ds(r0, c), cs].astype(F32)
            v = va_ref[pl.ds(r0, c), cs]
            f = lb + (1.0 - lb) * jax.nn.sigmoid(hf_ref[pl.ds(r0, c), cs])
            k = 1.0 - f
            g = jnp.log(f)
            g1 = g.astype(BF16)
            rem = g - g1.astype(F32)
            g2 = rem.astype(BF16)
            g3 = (rem - g2.astype(F32)).astype(BF16)
            gc = (jnp.dot(tri, g1, preferred_element_type=F32)
                  + jnp.dot(tri, g2, preferred_element_type=F32)
                  + jnp.dot(tri, g3, preferred_element_type=F32))
            g_last = gc[c - 1:c, :]
            st = state_ref[hd]
            qi = (q * jnp.exp(gc)).astype(BF16)
            o = lax.dot_general(qi, st.astype(BF16), (((1,), (1,)), ((), ())), preferred_element_type=F32)
            ks = (k * jnp.exp(g_last - gc)).astype(BF16)
            upd = lax.dot_general(v, ks, (((0,), (0,)), ((), ())), preferred_element_type=F32)
            state_ref[hd] = st * jnp.exp(g_last) + upd
            a = jnp.where(diag, jnp.sum(q * k, axis=-1, keepdims=True), 0.0)
            for lv, m in enumerate(levels):
                if m >= 8:
                    bnd = jnp.concatenate(
                        [jnp.broadcast_to(gc[p * 2 * m + m - 1:p * 2 * m + m, :], (2 * m, HEAD_DIM))
                         for p in range(c // (2 * m))], axis=0)
                else:
                    g3d = gc.reshape(c // 8, 8, HEAD_DIM)
                    bnd3 = None
                    for p in range(8 // (2 * m)):
                        src = jnp.broadcast_to(g3d[:, p * 2 * m + m - 1:p * 2 * m + m, :], (c // 8, 8, HEAD_DIM))
                        bnd3 = src if bnd3 is None else jnp.where(sub // (2 * m) == p, src, bnd3)
                    bnd = bnd3.reshape(c, HEAD_DIM)
                ex = jnp.exp((gc - bnd) * sgns[lv])
                ql = (q * ex).astype(BF16)
                kl = (k * ex).astype(BF16)
                p_l = lax.dot_general(ql, kl, (((1,), (1,)), ((), ())), preferred_element_type=F32)
                a = jnp.where(sels[lv], p_l, a)
            o = o + jnp.dot(a.astype(BF16), v, preferred_element_type=F32)
            o = _rms_rows(o) * on_ref[:, cs] * gate_ref[pl.ds(r0, c), cs].astype(F32)
            o_ref[pl.ds(r0, c), cs] = o.astype(BF16)
        return carry

    lax.fori_loop(0, HG_TB // c, chunk_body, 0)


def _hgrn(layer, qa, hf, va, gate, lb_logits, out_norm, b, t):
    nt = t // HG_TB
    tile = lambda: pl.BlockSpec((HG_TB, HG_W), lambda bi, i: (bi * nt + i, 0))
    return pl.pallas_call(
        functools.partial(_hgrn_kernel, layer),
        grid=(b, nt),
        in_specs=[tile(), tile(), tile(), tile(), _resident((DEPTH, HG_W)), _resident((1, HG_W))],
        out_specs=tile(),
        out_shape=jax.ShapeDtypeStruct((b * t, HG_W), BF16),
        scratch_shapes=[pltpu.VMEM((HG_HEADS, HEAD_DIM, HEAD_DIM), F32)],
        compiler_params=pltpu.CompilerParams(
            dimension_semantics=("parallel", "arbitrary"), vmem_limit_bytes=VMEM_LIMIT_BYTES_V7X),
        name="hgrn",
    )(qa, hf, va, gate, lb_logits, out_norm.reshape(1, HG_W))


def _attn_kernel(*refs):
    ins, o_ref, osc_ref, lsc_ref = refs[:15], refs[15], refs[16], refs[17]
    qi = lax.broadcasted_iota(jnp.int32, (ATT_BACK, 2 * ATT_BACK), 0)
    ki = lax.broadcasted_iota(jnp.int32, (ATT_BACK, 2 * ATT_BACK), 1)
    band = (ki >= qi) & (ki <= qi + ATT_BACK)
    first_key = jnp.where(pl.program_id(2) == 0, ATT_BACK, 0)
    band_first = band & (ki >= first_key)
    for g, d in enumerate(ATT_DILATIONS):
        q_ref, kc_ref, kp_ref, vc_ref, vp_ref = ins[5 * g:5 * g + 5]
        nblk = ATT_TT // d // ATT_BACK
        for r in range(d):
            for j in range(nblk):
                q = q_ref[r, j * ATT_BACK:(j + 1) * ATT_BACK, :]
                if j == 0:
                    kk = jnp.concatenate([kp_ref[r], kc_ref[r, 0:ATT_BACK, :]], axis=0)
                    vv = jnp.concatenate([vp_ref[r], vc_ref[r, 0:ATT_BACK, :]], axis=0)
                    mask = band_first
                else:
                    kk = kc_ref[r, (j - 1) * ATT_BACK:(j + 1) * ATT_BACK, :]
                    vv = vc_ref[r, (j - 1) * ATT_BACK:(j + 1) * ATT_BACK, :]
                    mask = band
                s = lax.dot_general(q, kk, (((1,), (1,)), ((), ())), preferred_element_type=F32)
                s = jnp.where(mask, s, -jnp.inf)
                mx = jnp.max(s, axis=-1, keepdims=True)
                p = jnp.exp(s - mx)
                den = jnp.sum(p, axis=-1, keepdims=True)
                o = jnp.dot(p.astype(BF16), vv, preferred_element_type=F32) / den
                lse = mx + jnp.log(den)
                dst = pl.ds(j * ATT_BACK * d + r, ATT_BACK, stride=d) if d > 1 else pl.ds(j * ATT_BACK, ATT_BACK)
                osc_ref[g, dst, :] = o
                lsc_ref[g, dst, :] = jnp.broadcast_to(lse, (ATT_BACK, HEAD_DIM))
    for j in range(ATT_TT // ATT_BACK):
        rs = slice(j * ATT_BACK, (j + 1) * ATT_BACK)
        l0, l1, l2 = lsc_ref[0, rs, :], lsc_ref[1, rs, :], lsc_ref[2, rs, :]
        mx = jnp.maximum(jnp.maximum(l0, l1), l2)
        w0, w1, w2 = jnp.exp(l0 - mx), jnp.exp(l1 - mx), jnp.exp(l2 - mx)
        o = (w0 * osc_ref[0, rs, :] + w1 * osc_ref[1, rs, :] + w2 * osc_ref[2, rs, :]) / (w0 + w1 + w2)
        o_ref[rs, :] = o.astype(BF16)


def _attn(qkv, b, t):
    nt = t // ATT_TT
    in_specs, args = [], []
    for g, d in enumerate(ATT_DILATIONS):
        rows = ATT_TT // d
        nb = rows // ATT_BACK
        cur = lambda rows=rows, d=d: pl.BlockSpec((None, d, rows, HEAD_DIM), lambda bi, h, i: (bi, 0, i, h))
        prev = lambda nb=nb, d=d: pl.BlockSpec(
            (None, d, ATT_BACK, HEAD_DIM), lambda bi, h, i: (bi, 0, jnp.maximum(i * nb - 1, 0), h))
        q, k, v = qkv[g]
        in_specs += [cur(), cur(), prev(), cur(), prev()]
        args += [q, k, k, v, v]
    return pl.pallas_call(
        _attn_kernel,
        grid=(b, ATT_HEADS, nt),
        in_specs=in_specs,
        out_specs=pl.BlockSpec((None, ATT_TT, HEAD_DIM), lambda bi, h, i: (bi, i, h)),
        out_shape=jax.ShapeDtypeStruct((b, t, ATT_GW), BF16),
        scratch_shapes=[pltpu.VMEM((ATT_GROUPS, ATT_TT, HEAD_DIM), F32),
                        pltpu.VMEM((ATT_GROUPS, ATT_TT, HEAD_DIM), F32)],
        compiler_params=pltpu.CompilerParams(
            dimension_semantics=("parallel", "parallel", "arbitrary"), vmem_limit_bytes=VMEM_LIMIT_BYTES_V7X),
        name="attn",
    )(*args)


def _merge_kernel(x_ref, oa_ref, ob_ref, sga_ref, sgb_ref, wa_ref, wb_ref, wo_ref, o_ref):
    ya = jnp.dot(oa_ref[...], wa_ref[...], preferred_element_type=F32)
    yb = jnp.dot(ob_ref[...], wb_ref[...], preferred_element_type=F32)
    merged = sga_ref[...].astype(F32) * ya + sgb_ref[...].astype(F32) * yb
    o_ref[...] = x_ref[...] + jnp.dot(merged.astype(BF16), wo_ref[...], preferred_element_type=F32)


def _merge(x2d, oa, ob, sga, sgb, wa, wb, wo):
    n = x2d.shape[0]
    tile = lambda w: pl.BlockSpec((MERGE_TM, w), lambda i: (i, 0))
    return pl.pallas_call(
        _merge_kernel,
        grid=(n // MERGE_TM,),
        in_specs=[tile(D_MODEL), tile(HG_W), tile(ATT_GW), tile(D_MODEL), tile(D_MODEL),
                  _resident((HG_W, D_MODEL)), _resident((ATT_GW, D_MODEL)), _resident((D_MODEL, D_MODEL))],
        out_specs=tile(D_MODEL),
        out_shape=jax.ShapeDtypeStruct((n, D_MODEL), F32),
        compiler_params=pltpu.CompilerParams(
            dimension_semantics=("parallel",), vmem_limit_bytes=VMEM_LIMIT_BYTES_V7X),
        name="merge",
    )(x2d, oa, ob, sga, sgb, wa, wb, wo)


def kernel(x, ffn1_norm, ffn1_w_in, ffn1_w_out, mix_norm, w_in, hgrn_lb_logits, hgrn_out_norm, attn_q_norm,
           attn_k_norm, w_branch_a, w_branch_b, w_out, ffn2_norm, ffn2_w_in, ffn2_w_out):
    b, t, _ = x.shape
    assert x.shape[2] == D_MODEL and t % ATT_TT == 0 and t % HG_TB == 0
    tables = _rope_tables(t)
    x2d = x.reshape(b * t, D_MODEL)
    for l in range(DEPTH):
        x2d = _ffn(x2d, ffn1_norm[l], ffn1_w_in[l].astype(BF16), ffn1_w_out[l].astype(BF16))

        w = w_in[l]
        w_a = jnp.concatenate([w[:, :COL_AQ], w[:, COL_GA:]], axis=1).astype(BF16)
        w_b = w[:, COL_AQ:COL_GA].astype(BF16)
        qa, hf, va, gate, sga, sgb = _proj_a(x2d, mix_norm[l], w_a)
        qkv = _proj_b(x2d.reshape(b, t, D_MODEL), mix_norm[l], w_b, attn_q_norm[l], attn_k_norm[l], tables)
        oa = _hgrn(l, qa, hf, va, gate, hgrn_lb_logits, hgrn_out_norm[l], b, t)
        ob = _attn([qkv[3 * g:3 * g + 3] for g in range(ATT_GROUPS)], b, t)
        x2d = _merge(x2d, oa, ob.reshape(b * t, ATT_GW), sga, sgb,
                     w_branch_a[l].astype(BF16), w_branch_b[l].astype(BF16), w_out[l].astype(BF16))

        x2d = _ffn(x2d, ffn2_norm[l], ffn2_w_in[l].astype(BF16), ffn2_w_out[l].astype(BF16))
    return x2d.reshape(b, t, D_MODEL)
```

```python
import functools

import jax
import jax.numpy as jnp
import numpy as np
from jax import lax
from jax.experimental import pallas as pl
from jax.experimental.pallas import tpu as pltpu

F32 = jnp.float32
BF16 = jnp.bfloat16

D_MODEL = 1024
DEPTH = 2
D_FF = 2816
HEAD_DIM = 128
HG_HEADS = D_MODEL // HEAD_DIM
HG_W = HG_HEADS * HEAD_DIM
ATT_DILATIONS = (1, 4, 16)
ATT_BACK = 128
ATT_GROUPS = 3
ATT_HEADS = 4
ATT_GW = ATT_HEADS * HEAD_DIM
ATT_W = ATT_GROUPS * ATT_GW
ROPE_THETA = 10000.0
EPS = 1e-6

COL_HQ, COL_HF, COL_HI, COL_HG = 0, HG_W, 2 * HG_W, 3 * HG_W
COL_AQ = 4 * HG_W
COL_AK = COL_AQ + ATT_W
COL_AV = COL_AK + ATT_W
COL_GA = COL_AV + ATT_W
COL_GB = COL_GA + D_MODEL
P_IN = COL_GB + D_MODEL

VMEM_LIMIT_BYTES_V7X = 56 * 1024 * 1024

FFN_TM = 512
FFN_FC = 256
PROJ_TM = 512
PROJ_NC = 512
HG_TB = 512
HG_C = 64
ATT_TT = ATT_BACK * ATT_DILATIONS[-1]
MERGE_TM = 512


def _resident(shape):
    nd = len(shape)
    return pl.BlockSpec(shape, lambda *_: (0,) * nd, pipeline_mode=pl.Buffered(1))


def _rms_rows(x):
    return x * lax.rsqrt(jnp.mean(x * x, axis=-1, keepdims=True) + EPS)


def _silu(x):
    return x * jax.nn.sigmoid(x)


def _ffn_kernel(x_ref, g_ref, win_ref, wout_ref, o_ref, act_ref):
    x = x_ref[...]
    h = (_rms_rows(x) * g_ref[...]).astype(BF16)
    for c in range(D_FF // FFN_FC):
        lo = c * FFN_FC
        a = jnp.dot(h, win_ref[:, lo:lo + FFN_FC], preferred_element_type=F32)
        b = jnp.dot(h, win_ref[:, D_FF + lo:D_FF + lo + FFN_FC], preferred_element_type=F32)
        act_ref[:, lo:lo + FFN_FC] = (_silu(a) * b).astype(BF16)
    y = jnp.dot(act_ref[...], wout_ref[...], preferred_element_type=F32)
    o_ref[...] = x + 0.5 * y


def _ffn(x2d, gain, w_in, w_out):
    n = x2d.shape[0]
    return pl.pallas_call(
        _ffn_kernel,
        grid=(n // FFN_TM,),
        in_specs=[
            pl.BlockSpec((FFN_TM, D_MODEL), lambda i: (i, 0)),
            _resident((1, D_MODEL)),
            _resident((D_MODEL, 2 * D_FF)),
            _resident((D_FF, D_MODEL)),
        ],
        out_specs=pl.BlockSpec((FFN_TM, D_MODEL), lambda i: (i, 0)),
        out_shape=jax.ShapeDtypeStruct((n, D_MODEL), F32),
        scratch_shapes=[pltpu.VMEM((FFN_TM, D_FF), BF16)],
        compiler_params=pltpu.CompilerParams(
            dimension_semantics=("parallel",), vmem_limit_bytes=VMEM_LIMIT_BYTES_V7X),
        name="ffn",
    )(x2d, gain.reshape(1, D_MODEL), w_in, w_out)


def _proj_a_kernel(x_ref, g_ref, w_ref, qa_ref, hf_ref, va_ref, gate_ref, sga_ref, sgb_ref):
    h = (_rms_rows(x_ref[...]) * g_ref[...]).astype(BF16)
    outs = (
        (qa_ref, _silu), (hf_ref, None), (va_ref, None), (gate_ref, _silu),
        (sga_ref, jax.nn.sigmoid), (sgb_ref, jax.nn.sigmoid))
    for s, (o_ref, fn) in enumerate(outs):
        for c in range(HG_W // PROJ_NC):
            lo = c * PROJ_NC
            y = jnp.dot(h, w_ref[:, s * HG_W + lo:s * HG_W + lo + PROJ_NC], preferred_element_type=F32)
            if fn is not None:
                y = fn(y)
            o_ref[:, lo:lo + PROJ_NC] = y.astype(o_ref.dtype)


def _proj_a(x2d, gain, w_a):
    n = x2d.shape[0]
    tile = lambda: pl.BlockSpec((PROJ_TM, HG_W), lambda i: (i, 0))
    bf = jax.ShapeDtypeStruct((n, HG_W), BF16)
    return pl.pallas_call(
        _proj_a_kernel,
        grid=(n // PROJ_TM,),
        in_specs=[tile(), _resident((1, D_MODEL)), _resident((D_MODEL, 6 * HG_W))],
        out_specs=[tile() for _ in range(6)],
        out_shape=[bf, jax.ShapeDtypeStruct((n, HG_W), F32), bf, bf, bf, bf],
        compiler_params=pltpu.CompilerParams(
            dimension_semantics=("parallel",), vmem_limit_bytes=VMEM_LIMIT_BYTES_V7X),
        name="proj_a",
    )(x2d, gain.reshape(1, D_MODEL), w_a)


def _rope(y, cos, sin_signed):
    return y * cos + pltpu.roll(y, HEAD_DIM // 2, axis=1) * sin_signed


def _proj_b_kernel(x_ref, g_ref, w_ref, qn_ref, kn_ref,
                   cos0_ref, sin0_ref, cos1_ref, sin1_ref, cos2_ref, sin2_ref,
                   q0_ref, k0_ref, v0_ref, q1_ref, k1_ref, v1_ref, q2_ref, k2_ref, v2_ref,
                   slab_ref):
    h = (_rms_rows(x_ref[...]) * g_ref[...]).astype(BF16)
    tabs = ((cos0_ref, sin0_ref), (cos1_ref, sin1_ref), (cos2_ref, sin2_ref))
    outs = ((q0_ref, k0_ref, v0_ref), (q1_ref, k1_ref, v1_ref), (q2_ref, k2_ref, v2_ref))
    scale = HEAD_DIM ** -0.5
    for g, d in enumerate(ATT_DILATIONS):
        cos_ref, sin_ref = tabs[g]
        rows = PROJ_TM // d
        for s in range(3):
            o_ref = outs[g][s]
            col = s * ATT_W + g * ATT_GW
            y = jnp.dot(h, w_ref[:, col:col + ATT_GW], preferred_element_type=F32)
            for hd in range(ATT_HEADS):
                cs = slice(hd * HEAD_DIM, (hd + 1) * HEAD_DIM)
                if d > 1:
                    slab = slab_ref.at[(g - 1) * 3 + s, hd]
                    slab[...] = y[:, cs]
                    yh = jnp.concatenate([slab[pl.ds(r, rows, stride=d), :] for r in range(d)], axis=0)
                else:
                    yh = y[:, cs]
                if s < 2:
                    gain = (qn_ref if s == 0 else kn_ref)[g:g + 1, :]
                    yh = _rope(_rms_rows(yh) * gain, cos_ref[...].reshape(PROJ_TM, HEAD_DIM),
                               sin_ref[...].reshape(PROJ_TM, HEAD_DIM))
                    if s == 0:
                        yh = yh * scale
                o_ref[0, :, :, cs] = yh.astype(BF16).reshape(d, rows, HEAD_DIM)


def _proj_b(x3d, gain, w_b, qn, kn, tables):
    b, t, _ = x3d.shape
    nt = t // PROJ_TM
    in_specs = [
        pl.BlockSpec((None, PROJ_TM, D_MODEL), lambda bi, i: (bi, i, 0)),
        _resident((1, D_MODEL)),
        _resident((D_MODEL, 3 * ATT_W)),
        _resident((ATT_GROUPS, HEAD_DIM)),
        _resident((ATT_GROUPS, HEAD_DIM)),
    ]
    out_specs, out_shape = [], []
    for d in ATT_DILATIONS:
        for _ in range(2):
            in_specs.append(pl.BlockSpec((d, PROJ_TM // d, HEAD_DIM), lambda bi, i: (0, i, 0)))
    for d in ATT_DILATIONS:
        for _ in range(3):
            out_specs.append(pl.BlockSpec((1, d, PROJ_TM // d, ATT_GW), lambda bi, i: (bi, 0, i, 0)))
            out_shape.append(jax.ShapeDtypeStruct((b, d, t // d, ATT_GW), BF16))
    return pl.pallas_call(
        _proj_b_kernel,
        grid=(b, nt),
        in_specs=in_specs,
        out_specs=out_specs,
        out_shape=out_shape,
        scratch_shapes=[pltpu.VMEM((2 * 3, ATT_HEADS, PROJ_TM, HEAD_DIM), F32)],
        compiler_params=pltpu.CompilerParams(
            dimension_semantics=("parallel", "parallel"), vmem_limit_bytes=VMEM_LIMIT_BYTES_V7X),
        name="proj_b",
    )(x3d, gain.reshape(1, D_MODEL), w_b, qn, kn, *tables)


def _rope_tables(t):
    pos = jnp.arange(t, dtype=F32)
    inv = ROPE_THETA ** (-jnp.arange(0, HEAD_DIM, 2, dtype=F32) / HEAD_DIM)
    ang = pos[:, None] * inv[None, :]
    cos = jnp.concatenate([jnp.cos(ang), jnp.cos(ang)], axis=-1)
    sin = jnp.concatenate([-jnp.sin(ang), jnp.sin(ang)], axis=-1)
    tables = []
    for d in ATT_DILATIONS:
        for tab in (cos, sin):
            tables.append(tab.reshape(t // d, d, HEAD_DIM).transpose(1, 0, 2))
    return tables


def _hgrn_levels(c):
    levels = []
    m = 1
    while m < c:
        levels.append(m)
        m *= 2
    return levels


def _hgrn_kernel(layer, qa_ref, hf_ref, va_ref, gate_ref, lbl_ref, on_ref, o_ref, state_ref):
    c = HG_C

    @pl.when(pl.program_id(1) == 0)
    def _():
        state_ref[...] = jnp.zeros_like(state_ref)

    logits = lbl_ref[...]
    e = jnp.exp(logits - jnp.max(logits, axis=0, keepdims=True))
    sm = e / jnp.sum(e, axis=0, keepdims=True)
    lb_all = jnp.sum(sm[1:layer + 1], axis=0, keepdims=True) if layer > 0 else jnp.zeros((1, HG_W), F32)

    row = lax.broadcasted_iota(jnp.int32, (c, c), 0)
    colm = lax.broadcasted_iota(jnp.int32, (c, c), 1)
    tri = (colm <= row).astype(BF16)
    rix = lax.broadcasted_iota(jnp.int32, (c, HG_W), 0)
    sub = lax.broadcasted_iota(jnp.int32, (c // 8, 8, HG_W), 1)
    levels = _hgrn_levels(c)
    diag = row == colm
    sgns = [jnp.where(rix % (2 * m) >= m, 1.0, -1.0).astype(F32) for m in levels]
    sels = [((row // (2 * m)) == (colm // (2 * m))) & ((row % (2 * m)) >= m) & ((colm % (2 * m)) < m)
            for m in levels]
    heads = [slice(hd * HEAD_DIM, (hd + 1) * HEAD_DIM) for hd in range(HG_HEADS)]
    nt_dims = (((1,), (1,)), ((), ()))

    def chunk_body(ci, carry):
        rows = pl.ds(pl.multiple_of(ci * c, c), c)
        qb = qa_ref[rows, :]
        vb = va_ref[rows, :]
        f = lb_all + (1.0 - lb_all) * jax.nn.sigmoid(hf_ref[rows, :])
        k = 1.0 - f
        kb = k.astype(BF16)
        g = jnp.log2(f)
        g1 = g.astype(BF16)
        rem = g - g1.astype(F32)
        g2 = rem.astype(BF16)
        g3 = (rem - g2.astype(F32)).astype(BF16)
        gc = (jnp.dot(tri, g1, preferred_element_type=F32)
              + jnp.dot(tri, g2, preferred_element_type=F32)
              + jnp.dot(tri, g3, preferred_element_type=F32))
        g_last = gc[c - 1:c, :]
        qi = qb * jnp.exp2(gc).astype(BF16)
        ks = kb * jnp.exp2(g_last - gc).astype(BF16)
        decay = jnp.exp2(g_last)
        qk = qb.astype(F32) * k
        outs, accs = [], []
        for hd, cs in enumerate(heads):
            st = state_ref[hd]
            outs.append(lax.dot_general(qi[:, cs], st.astype(BF16), nt_dims, preferred_element_type=F32))
            upd = lax.dot_general(vb[:, cs], ks[:, cs], (((0,), (0,)), ((), ())), preferred_element_type=F32)
            state_ref[hd] = st * decay[:, cs] + upd
            accs.append(jnp.where(diag, jnp.sum(qk[:, cs], axis=-1, keepdims=True), 0.0))
        for lv, m in enumerate(levels):
            if m >= 8:
                bnd = jnp.concatenate(
                    [jnp.broadcast_to(gc[p * 2 * m + m - 1:p * 2 * m + m, :], (2 * m, HG_W))
                     for p in range(c // (2 * m))], axis=0)
            else:
                g3d = gc.reshape(c // 8, 8, HG_W)
                bnd3 = None
                for p in range(8 // (2 * m)):
                    src = jnp.broadcast_to(g3d[:, p * 2 * m + m - 1:p * 2 * m + m, :], (c // 8, 8, HG_W))
                    bnd3 = src if bnd3 is None else jnp.where(sub // (2 * m) == p, src, bnd3)
                bnd = bnd3.reshape(c, HG_W)
            ex = jnp.exp2((gc - bnd) * sgns[lv]).astype(BF16)
            ql = qb * ex
            kl = kb * ex
            for hd, cs in enumerate(heads):
                p_l = lax.dot_general(ql[:, cs], kl[:, cs], nt_dims, preferred_element_type=F32)
                accs[hd] = jnp.where(sels[lv], p_l, accs[hd])
        for hd, cs in enumerate(heads):
            o = outs[hd] + jnp.dot(accs[hd].astype(BF16), vb[:, cs], preferred_element_type=F32)
            o = _rms_rows(o) * on_ref[:, cs] * gate_ref[rows, cs].astype(F32)
            o_ref[rows, cs] = o.astype(BF16)
        return carry

    lax.fori_loop(0, HG_TB // c, chunk_body, 0)


def _hgrn(layer, qa, hf, va, gate, lb_logits, out_norm, b, t):
    nt = t // HG_TB
    tile = lambda: pl.BlockSpec((HG_TB, HG_W), lambda bi, i: (bi * nt + i, 0))
    return pl.pallas_call(
        functools.partial(_hgrn_kernel, layer),
        grid=(b, nt),
        in_specs=[tile(), tile(), tile(), tile(), _resident((DEPTH, HG_W)), _resident((1, HG_W))],
        out_specs=tile(),
        out_shape=jax.ShapeDtypeStruct((b * t, HG_W), BF16),
        scratch_shapes=[pltpu.VMEM((HG_HEADS, HEAD_DIM, HEAD_DIM), F32)],
        compiler_params=pltpu.CompilerParams(
            dimension_semantics=("parallel", "arbitrary"), vmem_limit_bytes=VMEM_LIMIT_BYTES_V7X),
        name="hgrn",
    )(qa, hf, va, gate, lb_logits, out_norm.reshape(1, HG_W))


def _attn_kernel(*refs):
    ins, o_ref, osc_ref, lsc_ref = refs[:15], refs[15], refs[16], refs[17]
    qi = lax.broadcasted_iota(jnp.int32, (ATT_BACK, 2 * ATT_BACK), 0)
    ki = lax.broadcasted_iota(jnp.int32, (ATT_BACK, 2 * ATT_BACK), 1)
    band = (ki >= qi) & (ki <= qi + ATT_BACK)
    first_key = jnp.where(pl.program_id(2) == 0, ATT_BACK, 0)
    band_first = band & (ki >= first_key)
    for g, d in enumerate(ATT_DILATIONS):
        q_ref, kc_ref, kp_ref, vc_ref, vp_ref = ins[5 * g:5 * g + 5]
        nblk = ATT_TT // d // ATT_BACK
        for r in range(d):
            for j in range(nblk):
                q = q_ref[r, j * ATT_BACK:(j + 1) * ATT_BACK, :]
                if j == 0:
                    kk = jnp.concatenate([kp_ref[r], kc_ref[r, 0:ATT_BACK, :]], axis=0)
                    vv = jnp.concatenate([vp_ref[r], vc_ref[r, 0:ATT_BACK, :]], axis=0)
                    mask = band_first
                else:
                    kk = kc_ref[r, (j - 1) * ATT_BACK:(j + 1) * ATT_BACK, :]
                    vv = vc_ref[r, (j - 1) * ATT_BACK:(j + 1) * ATT_BACK, :]
                    mask = band
                s = lax.dot_general(q, kk, (((1,), (1,)), ((), ())), preferred_element_type=F32)
                s = jnp.where(mask, s, -jnp.inf)
                mx = jnp.max(s, axis=-1, keepdims=True)
                p = jnp.exp(s - mx)
                den = jnp.sum(p, axis=-1, keepdims=True)
                o = jnp.dot(p.astype(BF16), vv, preferred_element_type=F32) / den
                lse = mx + jnp.log(den)
                dst = pl.ds(j * ATT_BACK * d + r, ATT_BACK, stride=d) if d > 1 else pl.ds(j * ATT_BACK, ATT_BACK)
                osc_ref[g, dst, :] = o
                lsc_ref[g, dst, :] = jnp.broadcast_to(lse, (ATT_BACK, HEAD_DIM))
    for j in range(ATT_TT // ATT_BACK):
        rs = slice(j * ATT_BACK, (j + 1) * ATT_BACK)
        l0, l1, l2 = lsc_ref[0, rs, :], lsc_ref[1, rs, :], lsc_ref[2, rs, :]
        mx = jnp.maximum(jnp.maximum(l0, l1), l2)
        w0, w1, w2 = jnp.exp(l0 - mx), jnp.exp(l1 - mx), jnp.exp(l2 - mx)
        o = (w0 * osc_ref[0, rs, :] + w1 * osc_ref[1, rs, :] + w2 * osc_ref[2, rs, :]) / (w0 + w1 + w2)
        o_ref[rs, :] = o.astype(BF16)


def _attn(qkv, b, t):
    nt = t // ATT_TT
    in_specs, args = [], []
    for g, d in enumerate(ATT_DILATIONS):
        rows = ATT_TT // d
        nb = rows // ATT_BACK
        cur = lambda rows=rows, d=d: pl.BlockSpec((None, d, rows, HEAD_DIM), lambda bi, h, i: (bi, 0, i, h))
        prev = lambda nb=nb, d=d: pl.BlockSpec(
            (None, d, ATT_BACK, HEAD_DIM), lambda bi, h, i: (bi, 0, jnp.maximum(i * nb - 1, 0), h))
        q, k, v = qkv[g]
        in_specs += [cur(), cur(), prev(), cur(), prev()]
        args += [q, k, k, v, v]
    return pl.pallas_call(
        _attn_kernel,
        grid=(b, ATT_HEADS, nt),
        in_specs=in_specs,
        out_specs=pl.BlockSpec((None, ATT_TT, HEAD_DIM), lambda bi, h, i: (bi, i, h)),
        out_shape=jax.ShapeDtypeStruct((b, t, ATT_GW), BF16),
        scratch_shapes=[pltpu.VMEM((ATT_GROUPS, ATT_TT, HEAD_DIM), F32),
                        pltpu.VMEM((ATT_GROUPS, ATT_TT, HEAD_DIM), F32)],
        compiler_params=pltpu.CompilerParams(
            dimension_semantics=("parallel", "parallel", "arbitrary"), vmem_limit_bytes=VMEM_LIMIT_BYTES_V7X),
        name="attn",
    )(*args)


def _merge_kernel(x_ref, oa_ref, ob_ref, sga_ref, sgb_ref, wa_ref, wb_ref, wo_ref, o_ref):
    ya = jnp.dot(oa_ref[...], wa_ref[...], preferred_element_type=F32)
    yb = jnp.dot(ob_ref[...], wb_ref[...], preferred_element_type=F32)
    merged = sga_ref[...].astype(F32) * ya + sgb_ref[...].astype(F32) * yb
    o_ref[...] = x_ref[...] + jnp.dot(merged.astype(BF16), wo_ref[...], preferred_element_type=F32)


def _merge(x2d, oa, ob, sga, sgb, wa, wb, wo):
    n = x2d.shape[0]
    tile = lambda w: pl.BlockSpec((MERGE_TM, w), lambda i: (i, 0))
    return pl.pallas_call(
        _merge_kernel,
        grid=(n // MERGE_TM,),
        in_specs=[tile(D_MODEL), tile(HG_W), tile(ATT_GW), tile(D_MODEL), tile(D_MODEL),
                  _resident((HG_W, D_MODEL)), _resident((ATT_GW, D_MODEL)), _resident((D_MODEL, D_MODEL))],
        out_specs=tile(D_MODEL),
        out_shape=jax.ShapeDtypeStruct((n, D_MODEL), F32),
        compiler_params=pltpu.CompilerParams(
            dimension_semantics=("parallel",), vmem_limit_bytes=VMEM_LIMIT_BYTES_V7X),
        name="merge",
    )(x2d, oa, ob, sga, sgb, wa, wb, wo)


def kernel(x, ffn1_norm, ffn1_w_in, ffn1_w_out, mix_norm, w_in, hgrn_lb_logits, hgrn_out_norm, attn_q_norm,
           attn_k_norm, w_branch_a, w_branch_b, w_out, ffn2_norm, ffn2_w_in, ffn2_w_out):
    b, t, _ = x.shape
    assert x.shape[2] == D_MODEL and t % ATT_TT == 0 and t % HG_TB == 0
    tables = _rope_tables(t)
    x2d = x.reshape(b * t, D_MODEL)
    for l in range(DEPTH):
        x2d = _ffn(x2d, ffn1_norm[l], ffn1_w_in[l].astype(BF16), ffn1_w_out[l].astype(BF16))

        w = w_in[l]
        w_a = jnp.concatenate([w[:, :COL_AQ], w[:, COL_GA:]], axis=1).astype(BF16)
        w_b = w[:, COL_AQ:COL_GA].astype(BF16)
        qa, hf, va, gate, sga, sgb = _proj_a(x2d, mix_norm[l], w_a)
        qkv = _proj_b(x2d.reshape(b, t, D_MODEL), mix_norm[l], w_b, attn_q_norm[l], attn_k_norm[l], tables)
        oa = _hgrn(l, qa, hf, va, gate, hgrn_lb_logits, hgrn_out_norm[l], b, t)
        ob = _attn([qkv[3 * g:3 * g + 3] for g in range(ATT_GROUPS)], b, t)
        x2d = _merge(x2d, oa, ob.reshape(b * t, ATT_GW), sga, sgb,
                     w_branch_a[l].astype(BF16), w_branch_b[l].astype(BF16), w_out[l].astype(BF16))

        x2d = _ffn(x2d, ffn2_norm[l], ffn2_w_in[l].astype(BF16), ffn2_w_out[l].astype(BF16))
    return x2d.reshape(b, t, D_MODEL)
```

```python
import functools
import math

import jax
import jax.numpy as jnp
import numpy as np
from jax import lax
from jax.experimental import pallas as pl
from jax.experimental.pallas import tpu as pltpu

F32 = jnp.float32
BF16 = jnp.bfloat16

D_MODEL = 1024
DEPTH = 2
D_FF = 2816
HEAD_DIM = 128
HG_HEADS = D_MODEL // HEAD_DIM
HG_W = HG_HEADS * HEAD_DIM
ATT_DILATIONS = (1, 4, 16)
ATT_BACK = 128
ATT_GROUPS = 3
ATT_HEADS = 4
ATT_GW = ATT_HEADS * HEAD_DIM
ATT_W = ATT_GROUPS * ATT_GW
ROPE_THETA = 10000.0
EPS = 1e-6

COL_HQ, COL_HF, COL_HI, COL_HG = 0, HG_W, 2 * HG_W, 3 * HG_W
COL_AQ = 4 * HG_W
COL_AK = COL_AQ + ATT_W
COL_AV = COL_AK + ATT_W
COL_GA = COL_AV + ATT_W
COL_GB = COL_GA + D_MODEL
P_IN = COL_GB + D_MODEL

VMEM_LIMIT_BYTES_V7X = 56 * 1024 * 1024

FFN_TM = 512
FFN_FC = 256
PROJ_TM = 512
PROJ_NC = 512
HG_TB = 512
HG_C = 128
ATT_TT = ATT_BACK * ATT_DILATIONS[-1]
MERGE_TM = 512


def _resident(shape):
    nd = len(shape)
    return pl.BlockSpec(shape, lambda *_: (0,) * nd, pipeline_mode=pl.Buffered(1))


def _layer_block(shape, layer, col_block=0):
    return pl.BlockSpec((None,) + tuple(shape), lambda *_: (layer, 0, col_block), pipeline_mode=pl.Buffered(1))


def _rms_rows(x):
    return x * lax.rsqrt(jnp.mean(x * x, axis=-1, keepdims=True) + EPS)


def _silu(x):
    return x * jax.nn.sigmoid(x)


def _ffn_kernel(x_ref, g_ref, win_ref, wout_ref, o_ref, act_ref):
    x = x_ref[...]
    h = (_rms_rows(x) * g_ref[...]).astype(BF16)
    for c in range(D_FF // FFN_FC):
        lo = c * FFN_FC
        a = jnp.dot(h, win_ref[:, lo:lo + FFN_FC], preferred_element_type=F32)
        b = jnp.dot(h, win_ref[:, D_FF + lo:D_FF + lo + FFN_FC], preferred_element_type=F32)
        act_ref[:, lo:lo + FFN_FC] = (_silu(a) * b).astype(BF16)
    y = jnp.dot(act_ref[...], wout_ref[...], preferred_element_type=F32)
    o_ref[...] = x + 0.5 * y


def _ffn(x2d, gain, w_in, w_out, layer):
    n = x2d.shape[0]
    return pl.pallas_call(
        _ffn_kernel,
        grid=(n // FFN_TM,),
        in_specs=[
            pl.BlockSpec((FFN_TM, D_MODEL), lambda i: (i, 0)),
            _resident((1, D_MODEL)),
            _layer_block((D_MODEL, 2 * D_FF), layer),
            _layer_block((D_FF, D_MODEL), layer),
        ],
        out_specs=pl.BlockSpec((FFN_TM, D_MODEL), lambda i: (i, 0)),
        out_shape=jax.ShapeDtypeStruct((n, D_MODEL), F32),
        scratch_shapes=[pltpu.VMEM((FFN_TM, D_FF), BF16)],
        compiler_params=pltpu.CompilerParams(
            dimension_semantics=("parallel",), vmem_limit_bytes=VMEM_LIMIT_BYTES_V7X),
        name="ffn",
    )(x2d, gain.reshape(1, D_MODEL), w_in, w_out)


def _proj_a_kernel(x_ref, g_ref, wh_ref, *refs):
    n_gate = 2 * D_MODEL // PROJ_NC
    wg_refs, (qa_ref, hf_ref, va_ref, gate_ref, sga_ref, sgb_ref) = refs[:n_gate], refs[n_gate:]
    h = (_rms_rows(x_ref[...]) * g_ref[...]).astype(BF16)
    outs = (
        (qa_ref, _silu), (hf_ref, None), (va_ref, None), (gate_ref, _silu),
        (sga_ref, jax.nn.sigmoid), (sgb_ref, jax.nn.sigmoid))
    for s, (o_ref, fn) in enumerate(outs):
        for c in range(HG_W // PROJ_NC):
            lo = c * PROJ_NC
            if s < 4:
                w = wh_ref[:, s * HG_W + lo:s * HG_W + lo + PROJ_NC]
            else:
                w = wg_refs[(s - 4) * (D_MODEL // PROJ_NC) + c][...]
            y = jnp.dot(h, w, preferred_element_type=F32)
            if fn is not None:
                y = fn(y)
            o_ref[:, lo:lo + PROJ_NC] = y.astype(o_ref.dtype)


def _proj_a(x2d, gain, w_in, layer):
    n = x2d.shape[0]
    tile = lambda: pl.BlockSpec((PROJ_TM, HG_W), lambda i: (i, 0))
    bf = jax.ShapeDtypeStruct((n, HG_W), BF16)
    gate_blocks = [_layer_block((D_MODEL, PROJ_NC), layer, COL_GA // PROJ_NC + c)
                   for c in range(2 * D_MODEL // PROJ_NC)]
    return pl.pallas_call(
        _proj_a_kernel,
        grid=(n // PROJ_TM,),
        in_specs=[tile(), _resident((1, D_MODEL)), _layer_block((D_MODEL, 4 * HG_W), layer)] + gate_blocks,
        out_specs=[tile() for _ in range(6)],
        out_shape=[bf, jax.ShapeDtypeStruct((n, HG_W), F32), bf, bf, bf, bf],
        compiler_params=pltpu.CompilerParams(
            dimension_semantics=("parallel",), vmem_limit_bytes=VMEM_LIMIT_BYTES_V7X),
        name="proj_a",
    )(x2d, gain.reshape(1, D_MODEL), w_in, *([w_in] * len(gate_blocks)))


def _rope(y, cos, sin_signed):
    return y * cos + pltpu.roll(y, HEAD_DIM // 2, axis=1) * sin_signed


def _proj_b_kernel(x_ref, g_ref, qn_ref, kn_ref, *refs):
    w_refs, tab_refs, out_refs, slab_ref = refs[:9], refs[9:15], refs[15:24], refs[24]
    h = (_rms_rows(x_ref[...]) * g_ref[...]).astype(BF16)
    tabs = [tab_refs[2 * g:2 * g + 2] for g in range(ATT_GROUPS)]
    outs = [out_refs[3 * g:3 * g + 3] for g in range(ATT_GROUPS)]
    scale = HEAD_DIM ** -0.5 * math.log2(math.e)
    order = [(s, g) for s in (0, 1, 2) for g in reversed(range(ATT_GROUPS))]
    for s, g in order:
        d = ATT_DILATIONS[g]
        cos_ref, sin_ref = tabs[g]
        rows = PROJ_TM // d
        if True:
            o_ref = outs[g][s]
            y = jnp.dot(h, w_refs[s * ATT_GROUPS + g][...], preferred_element_type=F32)
            for hd in range(ATT_HEADS):
                cs = slice(hd * HEAD_DIM, (hd + 1) * HEAD_DIM)
                if d > 1:
                    slab = slab_ref.at[(g - 1) * 3 + s, hd]
                    slab[...] = y[:, cs]
                    yh = jnp.concatenate([slab[pl.ds(r, rows, stride=d), :] for r in range(d)], axis=0)
                else:
                    yh = y[:, cs]
                if s < 2:
                    gain = (qn_ref if s == 0 else kn_ref)[g:g + 1, :]
                    yh = _rope(_rms_rows(yh) * gain, cos_ref[...].reshape(PROJ_TM, HEAD_DIM),
                               sin_ref[...].reshape(PROJ_TM, HEAD_DIM))
                    if s == 0:
                        yh = yh * scale
                o_ref[0, :, :, cs] = yh.astype(BF16).reshape(d, rows, HEAD_DIM)


def _proj_b(x3d, gain, w_in, layer, qn, kn, tables):
    b, t, _ = x3d.shape
    nt = t // PROJ_TM
    in_specs = [
        pl.BlockSpec((None, PROJ_TM, D_MODEL), lambda bi, i: (bi, i, 0)),
        _resident((1, D_MODEL)),
        _resident((ATT_GROUPS, HEAD_DIM)),
        _resident((ATT_GROUPS, HEAD_DIM)),
    ]
    n_w = 3 * ATT_GROUPS
    in_specs += [_layer_block((D_MODEL, ATT_GW), layer, COL_AQ // ATT_GW + k) for k in range(n_w)]
    out_specs, out_shape = [], []
    for d in ATT_DILATIONS:
        for _ in range(2):
            in_specs.append(pl.BlockSpec((d, PROJ_TM // d, HEAD_DIM), lambda bi, i: (0, i, 0)))
    for d in ATT_DILATIONS:
        for _ in range(3):
            out_specs.append(pl.BlockSpec((1, d, PROJ_TM // d, ATT_GW), lambda bi, i: (bi, 0, i, 0)))
            out_shape.append(jax.ShapeDtypeStruct((b, d, t // d, ATT_GW), BF16))
    return pl.pallas_call(
        _proj_b_kernel,
        grid=(b, nt),
        in_specs=in_specs,
        out_specs=out_specs,
        out_shape=out_shape,
        scratch_shapes=[pltpu.VMEM((2 * 3, ATT_HEADS, PROJ_TM, HEAD_DIM), F32)],
        compiler_params=pltpu.CompilerParams(
            dimension_semantics=("parallel", "parallel"), vmem_limit_bytes=VMEM_LIMIT_BYTES_V7X),
        name="proj_b",
    )(x3d, gain.reshape(1, D_MODEL), qn, kn, *([w_in] * n_w), *tables)


def _rope_tables(t):
    pos = jnp.arange(t, dtype=F32)
    inv = ROPE_THETA ** (-jnp.arange(0, HEAD_DIM, 2, dtype=F32) / HEAD_DIM)
    ang = pos[:, None] * inv[None, :]
    cos = jnp.concatenate([jnp.cos(ang), jnp.cos(ang)], axis=-1)
    sin = jnp.concatenate([-jnp.sin(ang), jnp.sin(ang)], axis=-1)
    tables = []
    for d in ATT_DILATIONS:
        for tab in (cos, sin):
            tables.append(tab.reshape(t // d, d, HEAD_DIM).transpose(1, 0, 2))
    return tables


def _hgrn_levels(c):
    levels = []
    m = 1
    while m < c:
        levels.append(m)
        m *= 2
    return levels


def _hgrn_kernel(layer, qa_ref, hf_ref, va_ref, gate_ref, lbl_ref, on_ref, o_ref, state_ref):
    c = HG_C

    @pl.when(pl.program_id(1) == 0)
    def _():
        state_ref[...] = jnp.zeros_like(state_ref)

    logits = lbl_ref[...]
    e = jnp.exp(logits - jnp.max(logits, axis=0, keepdims=True))
    sm = e / jnp.sum(e, axis=0, keepdims=True)
    lb_all = jnp.sum(sm[1:layer + 1], axis=0, keepdims=True) if layer > 0 else jnp.zeros((1, HG_W), F32)

    row = lax.broadcasted_iota(jnp.int32, (c, c), 0)
    colm = lax.broadcasted_iota(jnp.int32, (c, c), 1)
    tri = (colm <= row).astype(BF16)
    rix = lax.broadcasted_iota(jnp.int32, (c, HG_W), 0)
    sub = lax.broadcasted_iota(jnp.int32, (c // 8, 8, HG_W), 1)
    levels = _hgrn_levels(c)
    nrb = c // 8
    rblk = lambda a, i: a[i * 8:(i + 1) * 8]
    diag = [rblk(row == colm, i) for i in range(nrb)]
    sgns = {m: jnp.where(rix % (2 * m) >= m, 1.0, -1.0).astype(F32) for m in levels if 1 < m < 16}
    sels = {}
    for m in levels:
        sel = ((row // (2 * m)) == (colm // (2 * m))) & ((row % (2 * m)) >= m) & ((colm % (2 * m)) < m)
        sels[m] = [rblk(sel, i) for i in range(nrb)]
    odd_row = (rix % 2) == 1
    heads = [slice(hd * HEAD_DIM, (hd + 1) * HEAD_DIM) for hd in range(HG_HEADS)]
    nt_dims = (((1,), (1,)), ((), ()))

    def chunk_body(ci, carry):
        rows = pl.ds(pl.multiple_of(ci * c, c), c)
        qb = qa_ref[rows, :]
        vb = va_ref[rows, :]
        f = lb_all + (1.0 - lb_all) * jax.nn.sigmoid(hf_ref[rows, :])
        k = 1.0 - f
        kb = k.astype(BF16)
        g = jnp.log2(f)
        g1 = g.astype(BF16)
        rem = g - g1.astype(F32)
        g2 = rem.astype(BF16)
        g3 = (rem - g2.astype(F32)).astype(BF16)
        gc = (jnp.dot(tri, g1, preferred_element_type=F32)
              + jnp.dot(tri, g2, preferred_element_type=F32)
              + jnp.dot(tri, g3, preferred_element_type=F32))
        g_last = gc[c - 1:c, :]
        qi = qb * jnp.exp2(gc).astype(BF16)
        ks = kb * jnp.exp2(g_last - gc).astype(BF16)
        decay = jnp.exp2(g_last)
        outs, accs = [], []
        for hd, cs in enumerate(heads):
            st = state_ref[hd]
            outs.append(lax.dot_general(qi[:, cs], st.astype(BF16), nt_dims, preferred_element_type=F32))
            upd = lax.dot_general(vb[:, cs], ks[:, cs], (((0,), (0,)), ((), ())), preferred_element_type=F32)
            state_ref[hd] = st * decay[:, cs] + upd
            p_d = lax.dot_general(qb[:, cs], kb[:, cs], nt_dims, preferred_element_type=F32)
            accs.append([jnp.where(diag[i], rblk(p_d, i), 0.0) for i in range(nrb)])
        for m in levels:
            if m >= 16:
                kparts, qparts, qblocks = [], [], []
                for p in range(c // (2 * m)):
                    lo, mid, hi = p * 2 * m, p * 2 * m + m, (p + 1) * 2 * m
                    gb = gc[mid - 1:mid, :]
                    kparts.append(kb[lo:mid] * jnp.exp2(gb - gc[lo:mid]).astype(BF16))
                    kparts.append(kb[mid:hi])
                    qparts.append(qb[mid:hi] * jnp.exp2(gc[mid:hi] - gb).astype(BF16))
                    qblocks += list(range(mid // 8, hi // 8))
                kl = jnp.concatenate(kparts, axis=0)
                ql = jnp.concatenate(qparts, axis=0)
            else:
                qblocks = list(range(nrb))
                if m == 1:
                    ql = qb * jnp.where(odd_row, f, 1.0).astype(BF16)
                    kl = kb
                else:
                    grp = max(8, 2 * m)
                    g3d = gc.reshape(c // grp, grp, HG_W)
                    bnd3 = None
                    for p in range(grp // (2 * m)):
                        src = jnp.broadcast_to(g3d[:, p * 2 * m + m - 1:p * 2 * m + m, :], (c // grp, grp, HG_W))
                        bnd3 = src if bnd3 is None else jnp.where(sub // (2 * m) == p, src, bnd3)
                    ex = jnp.exp2((gc - bnd3.reshape(c, HG_W)) * sgns[m]).astype(BF16)
                    ql = qb * ex
                    kl = kb * ex
            for hd, cs in enumerate(heads):
                p_l = lax.dot_general(ql[:, cs], kl[:, cs], nt_dims, preferred_element_type=F32)
                for n, i in enumerate(qblocks):
                    accs[hd][i] = jnp.where(sels[m][i], rblk(p_l, n), accs[hd][i])
        for hd, cs in enumerate(heads):
            a = jnp.concatenate(accs[hd], axis=0).astype(BF16)
            o = outs[hd] + jnp.dot(a, vb[:, cs], preferred_element_type=F32)
            o = _rms_rows(o) * on_ref[:, cs] * gate_ref[rows, cs].astype(F32)
            o_ref[rows, cs] = o.astype(BF16)
        return carry

    lax.fori_loop(0, HG_TB // c, chunk_body, 0, unroll=True)


def _hgrn(layer, qa, hf, va, gate, lb_logits, out_norm, b, t):
    nt = t // HG_TB
    tile = lambda: pl.BlockSpec((HG_TB, HG_W), lambda bi, i: (bi * nt + i, 0))
    return pl.pallas_call(
        functools.partial(_hgrn_kernel, layer),
        grid=(b, nt),
        in_specs=[tile(), tile(), tile(), tile(), _resident((DEPTH, HG_W)), _resident((1, HG_W))],
        out_specs=tile(),
        out_shape=jax.ShapeDtypeStruct((b * t, HG_W), BF16),
        scratch_shapes=[pltpu.VMEM((HG_HEADS, HEAD_DIM, HEAD_DIM), F32)],
        compiler_params=pltpu.CompilerParams(
            dimension_semantics=("parallel", "arbitrary"), vmem_limit_bytes=VMEM_LIMIT_BYTES_V7X),
        name="hgrn",
    )(qa, hf, va, gate, lb_logits, out_norm.reshape(1, HG_W))


def _attn_kernel(*refs):
    ins, o_ref, osc_ref, lsc_ref = refs[:15], refs[15], refs[16], refs[17]
    qi = lax.broadcasted_iota(jnp.int32, (ATT_BACK, 2 * ATT_BACK), 0)
    ki = lax.broadcasted_iota(jnp.int32, (ATT_BACK, 2 * ATT_BACK), 1)
    band = (ki >= qi) & (ki <= qi + ATT_BACK)
    first_key = jnp.where(pl.program_id(2) == 0, ATT_BACK, 0)
    band_first = band & (ki >= first_key)
    for g, d in enumerate(ATT_DILATIONS):
        q_ref, kc_ref, kp_ref, vc_ref, vp_ref = ins[5 * g:5 * g + 5]
        nblk = ATT_TT // d // ATT_BACK
        for r in range(d):
            for j in range(nblk):
                q = q_ref[r, j * ATT_BACK:(j + 1) * ATT_BACK, :]
                if j == 0:
                    kk = jnp.concatenate([kp_ref[r], kc_ref[r, 0:ATT_BACK, :]], axis=0)
                    vv = jnp.concatenate([vp_ref[r], vc_ref[r, 0:ATT_BACK, :]], axis=0)
                    mask = band_first
                else:
                    kk = kc_ref[r, (j - 1) * ATT_BACK:(j + 1) * ATT_BACK, :]
                    vv = vc_ref[r, (j - 1) * ATT_BACK:(j + 1) * ATT_BACK, :]
                    mask = band
                s = lax.dot_general(q, kk, (((1,), (1,)), ((), ())), preferred_element_type=F32)
                s = jnp.where(mask, s, -jnp.inf)
                mx = jnp.max(s, axis=-1, keepdims=True)
                p = jnp.exp2(s - mx)
                den = jnp.sum(p, axis=-1, keepdims=True)
                o = jnp.dot(p.astype(BF16), vv, preferred_element_type=F32) / den
                lse = mx + jnp.log2(den)
                dst = pl.ds(j * ATT_BACK * d + r, ATT_BACK, stride=d) if d > 1 else pl.ds(j * ATT_BACK, ATT_BACK)
                osc_ref[g, dst, :] = o
                lsc_ref[g, dst, :] = jnp.broadcast_to(lse, (ATT_BACK, HEAD_DIM))
    for j in range(ATT_TT // ATT_BACK):
        rs = slice(j * ATT_BACK, (j + 1) * ATT_BACK)
        l0, l1, l2 = lsc_ref[0, rs, :], lsc_ref[1, rs, :], lsc_ref[2, rs, :]
        mx = jnp.maximum(jnp.maximum(l0, l1), l2)
        w0, w1, w2 = jnp.exp2(l0 - mx), jnp.exp2(l1 - mx), jnp.exp2(l2 - mx)
        o = (w0 * osc_ref[0, rs, :] + w1 * osc_ref[1, rs, :] + w2 * osc_ref[2, rs, :]) / (w0 + w1 + w2)
        o_ref[rs, :] = o.astype(BF16)


def _attn(qkv, b, t):
    nt = t // ATT_TT
    in_specs, args = [], []
    for g, d in enumerate(ATT_DILATIONS):
        rows = ATT_TT // d
        nb = rows // ATT_BACK
        cur = lambda rows=rows, d=d: pl.BlockSpec((None, d, rows, HEAD_DIM), lambda bi, h, i: (bi, 0, i, h))
        prev = lambda nb=nb, d=d: pl.BlockSpec(
            (None, d, ATT_BACK, HEAD_DIM), lambda bi, h, i: (bi, 0, jnp.maximum(i * nb - 1, 0), h))
        q, k, v = qkv[g]
        in_specs += [cur(), cur(), prev(), cur(), prev()]
        args += [q, k, k, v, v]
    return pl.pallas_call(
        _attn_kernel,
        grid=(b, ATT_HEADS, nt),
        in_specs=in_specs,
        out_specs=pl.BlockSpec((None, ATT_TT, HEAD_DIM), lambda bi, h, i: (bi, i, h)),
        out_shape=jax.ShapeDtypeStruct((b, t, ATT_GW), BF16),
        scratch_shapes=[pltpu.VMEM((ATT_GROUPS, ATT_TT, HEAD_DIM), F32),
                        pltpu.VMEM((ATT_GROUPS, ATT_TT, HEAD_DIM), F32)],
        compiler_params=pltpu.CompilerParams(
            dimension_semantics=("parallel", "parallel", "arbitrary"), vmem_limit_bytes=VMEM_LIMIT_BYTES_V7X),
        name="attn",
    )(*args)


def _merge_kernel(x_ref, oa_ref, ob_ref, sga_ref, sgb_ref, wa_ref, wb_ref, wo_ref, o_ref):
    ya = jnp.dot(oa_ref[...], wa_ref[...], preferred_element_type=F32)
    yb = jnp.dot(ob_ref[...], wb_ref[...], preferred_element_type=F32)
    merged = sga_ref[...].astype(F32) * ya + sgb_ref[...].astype(F32) * yb
    o_ref[...] = x_ref[...] + jnp.dot(merged.astype(BF16), wo_ref[...], preferred_element_type=F32)


def _merge(x2d, oa, ob, sga, sgb, wa, wb, wo, layer):
    n = x2d.shape[0]
    tile = lambda w: pl.BlockSpec((MERGE_TM, w), lambda i: (i, 0))
    return pl.pallas_call(
        _merge_kernel,
        grid=(n // MERGE_TM,),
        in_specs=[tile(D_MODEL), tile(HG_W), tile(ATT_GW), tile(D_MODEL), tile(D_MODEL),
                  _layer_block((HG_W, D_MODEL), layer), _layer_block((ATT_GW, D_MODEL), layer),
                  _layer_block((D_MODEL, D_MODEL), layer)],
        out_specs=tile(D_MODEL),
        out_shape=jax.ShapeDtypeStruct((n, D_MODEL), F32),
        compiler_params=pltpu.CompilerParams(
            dimension_semantics=("parallel",), vmem_limit_bytes=VMEM_LIMIT_BYTES_V7X),
        name="merge",
    )(x2d, oa, ob, sga, sgb, wa, wb, wo)


def kernel(x, ffn1_norm, ffn1_w_in, ffn1_w_out, mix_norm, w_in, hgrn_lb_logits, hgrn_out_norm, attn_q_norm,
           attn_k_norm, w_branch_a, w_branch_b, w_out, ffn2_norm, ffn2_w_in, ffn2_w_out):
    b, t, _ = x.shape
    assert x.shape[2] == D_MODEL and t % ATT_TT == 0 and t % HG_TB == 0
    assert COL_GA % PROJ_NC == 0 and COL_AQ % ATT_GW == 0
    tables = _rope_tables(t)
    ffn1_w_in, ffn1_w_out, w_in, w_branch_a, w_branch_b, w_out, ffn2_w_in, ffn2_w_out = (
        w.astype(BF16) for w in (ffn1_w_in, ffn1_w_out, w_in, w_branch_a, w_branch_b, w_out, ffn2_w_in, ffn2_w_out))
    x2d = x.reshape(b * t, D_MODEL)
    for l in range(DEPTH):
        x2d = _ffn(x2d, ffn1_norm[l], ffn1_w_in, ffn1_w_out, l)

        qa, hf, va, gate, sga, sgb = _proj_a(x2d, mix_norm[l], w_in, l)
        qkv = _proj_b(x2d.reshape(b, t, D_MODEL), mix_norm[l], w_in, l, attn_q_norm[l], attn_k_norm[l], tables)
        oa = _hgrn(l, qa, hf, va, gate, hgrn_lb_logits, hgrn_out_norm[l], b, t)
        ob = _attn([qkv[3 * g:3 * g + 3] for g in range(ATT_GROUPS)], b, t)
        x2d = _merge(x2d, oa, ob.reshape(b * t, ATT_GW), sga, sgb, w_branch_a, w_branch_b, w_out, l)

        x2d = _ffn(x2d, ffn2_norm[l], ffn2_w_in, ffn2_w_out, l)
    return x2d.reshape(b, t, D_MODEL)
```

```python
import functools
import math

import jax
import jax.numpy as jnp
import numpy as np
from jax import lax
from jax.experimental import pallas as pl
from jax.experimental.pallas import tpu as pltpu

F32 = jnp.float32
BF16 = jnp.bfloat16

D_MODEL = 1024
DEPTH = 2
D_FF = 2816
HEAD_DIM = 128
HG_HEADS = D_MODEL // HEAD_DIM
HG_W = HG_HEADS * HEAD_DIM
ATT_DILATIONS = (1, 4, 16)
ATT_BACK = 128
ATT_GROUPS = 3
ATT_HEADS = 4
ATT_GW = ATT_HEADS * HEAD_DIM
ATT_W = ATT_GROUPS * ATT_GW
ROPE_THETA = 10000.0
EPS = 1e-6

COL_HQ, COL_HF, COL_HI, COL_HG = 0, HG_W, 2 * HG_W, 3 * HG_W
COL_AQ = 4 * HG_W
COL_AK = COL_AQ + ATT_W
COL_AV = COL_AK + ATT_W
COL_GA = COL_AV + ATT_W
COL_GB = COL_GA + D_MODEL
P_IN = COL_GB + D_MODEL

VMEM_LIMIT_BYTES_V7X = 56 * 1024 * 1024

FFN_TM = 512
FFN_FC = 256
PROJ_TM = 512
PROJ_NC = 512
HG_TB = 512
HG_C = 128
ATT_TT = ATT_BACK * ATT_DILATIONS[-1]
MERGE_TM = 512


def _resident(shape):
    nd = len(shape)
    return pl.BlockSpec(shape, lambda *_: (0,) * nd, pipeline_mode=pl.Buffered(1))


def _layer_block(shape, layer, col_block=0):
    return pl.BlockSpec((None,) + tuple(shape), lambda *_: (layer, 0, col_block), pipeline_mode=pl.Buffered(1))


def _rms_rows(x):
    return x * lax.rsqrt(jnp.mean(x * x, axis=-1, keepdims=True) + EPS)


def _silu(x):
    return x * jax.nn.sigmoid(x)


def _ffn_kernel(x_ref, g_ref, win_ref, wout_ref, o_ref, act_ref):
    x = x_ref[...]
    h = (_rms_rows(x) * g_ref[...]).astype(BF16)
    for c in range(D_FF // FFN_FC):
        lo = c * FFN_FC
        a = jnp.dot(h, win_ref[:, lo:lo + FFN_FC], preferred_element_type=F32)
        b = jnp.dot(h, win_ref[:, D_FF + lo:D_FF + lo + FFN_FC], preferred_element_type=F32)
        act_ref[:, lo:lo + FFN_FC] = (_silu(a) * b).astype(BF16)
    y = jnp.dot(act_ref[...], wout_ref[...], preferred_element_type=F32)
    o_ref[...] = x + 0.5 * y


def _ffn(x2d, gain, w_in, w_out, layer):
    n = x2d.shape[0]
    return pl.pallas_call(
        _ffn_kernel,
        grid=(n // FFN_TM,),
        in_specs=[
            pl.BlockSpec((FFN_TM, D_MODEL), lambda i: (i, 0)),
            _resident((1, D_MODEL)),
            _layer_block((D_MODEL, 2 * D_FF), layer),
            _layer_block((D_FF, D_MODEL), layer),
        ],
        out_specs=pl.BlockSpec((FFN_TM, D_MODEL), lambda i: (i, 0)),
        out_shape=jax.ShapeDtypeStruct((n, D_MODEL), F32),
        scratch_shapes=[pltpu.VMEM((FFN_TM, D_FF), BF16)],
        compiler_params=pltpu.CompilerParams(
            dimension_semantics=("parallel",), vmem_limit_bytes=VMEM_LIMIT_BYTES_V7X),
        name="ffn",
    )(x2d, gain.reshape(1, D_MODEL), w_in, w_out)


def _proj_a_kernel(x_ref, g_ref, wh_ref, *refs):
    n_gate = 2 * D_MODEL // PROJ_NC
    wg_refs, (qa_ref, hf_ref, va_ref, gate_ref, sga_ref, sgb_ref) = refs[:n_gate], refs[n_gate:]
    h = (_rms_rows(x_ref[...]) * g_ref[...]).astype(BF16)
    outs = (
        (qa_ref, _silu), (hf_ref, None), (va_ref, None), (gate_ref, _silu),
        (sga_ref, jax.nn.sigmoid), (sgb_ref, jax.nn.sigmoid))
    for s, (o_ref, fn) in enumerate(outs):
        for c in range(HG_W // PROJ_NC):
            lo = c * PROJ_NC
            if s < 4:
                w = wh_ref[:, s * HG_W + lo:s * HG_W + lo + PROJ_NC]
            else:
                w = wg_refs[(s - 4) * (D_MODEL // PROJ_NC) + c][...]
            y = jnp.dot(h, w, preferred_element_type=F32)
            if fn is not None:
                y = fn(y)
            o_ref[:, lo:lo + PROJ_NC] = y.astype(o_ref.dtype)


def _proj_a(x2d, gain, w_in, layer):
    n = x2d.shape[0]
    tile = lambda: pl.BlockSpec((PROJ_TM, HG_W), lambda i: (i, 0))
    bf = jax.ShapeDtypeStruct((n, HG_W), BF16)
    gate_blocks = [_layer_block((D_MODEL, PROJ_NC), layer, COL_GA // PROJ_NC + c)
                   for c in range(2 * D_MODEL // PROJ_NC)]
    return pl.pallas_call(
        _proj_a_kernel,
        grid=(n // PROJ_TM,),
        in_specs=[tile(), _resident((1, D_MODEL)), _layer_block((D_MODEL, 4 * HG_W), layer)] + gate_blocks,
        out_specs=[tile() for _ in range(6)],
        out_shape=[bf, jax.ShapeDtypeStruct((n, HG_W), F32), bf, bf, bf, bf],
        compiler_params=pltpu.CompilerParams(
            dimension_semantics=("parallel",), vmem_limit_bytes=VMEM_LIMIT_BYTES_V7X),
        name="proj_a",
    )(x2d, gain.reshape(1, D_MODEL), w_in, *([w_in] * len(gate_blocks)))


def _rope(y, cos, sin_signed):
    return y * cos + pltpu.roll(y, HEAD_DIM // 2, axis=1) * sin_signed


def _proj_b_kernel(x_ref, g_ref, qn_ref, kn_ref, *refs):
    w_refs, tab_refs, out_refs, slab_ref = refs[:9], refs[9:15], refs[15:24], refs[24]
    h = (_rms_rows(x_ref[...]) * g_ref[...]).astype(BF16)
    tabs = [tab_refs[2 * g:2 * g + 2] for g in range(ATT_GROUPS)]
    outs = [out_refs[3 * g:3 * g + 3] for g in range(ATT_GROUPS)]
    scale = HEAD_DIM ** -0.5 * math.log2(math.e)
    order = [(s, g) for g in range(ATT_GROUPS) for s in (0, 1, 2)]
    for s, g in order:
        d = ATT_DILATIONS[g]
        cos_ref, sin_ref = tabs[g]
        rows = PROJ_TM // d
        if True:
            o_ref = outs[g][s]
            y = jnp.dot(h, w_refs[s * ATT_GROUPS + g][...], preferred_element_type=F32)
            for hd in range(ATT_HEADS):
                cs = slice(hd * HEAD_DIM, (hd + 1) * HEAD_DIM)
                if d > 1:
                    slab = slab_ref.at[(g - 1) * 3 + s, hd]
                    slab[...] = y[:, cs]
                    yh = jnp.concatenate([slab[pl.ds(r, rows, stride=d), :] for r in range(d)], axis=0)
                else:
                    yh = y[:, cs]
                if s < 2:
                    gain = (qn_ref if s == 0 else kn_ref)[g:g + 1, :]
                    yh = _rope(_rms_rows(yh) * gain, cos_ref[...].reshape(PROJ_TM, HEAD_DIM),
                               sin_ref[...].reshape(PROJ_TM, HEAD_DIM))
                    if s == 0:
                        yh = yh * scale
                o_ref[0, :, :, cs] = yh.astype(BF16).reshape(d, rows, HEAD_DIM)


def _proj_b(x3d, gain, w_in, layer, qn, kn, tables):
    b, t, _ = x3d.shape
    nt = t // PROJ_TM
    in_specs = [
        pl.BlockSpec((None, PROJ_TM, D_MODEL), lambda bi, i: (bi, i, 0)),
        _resident((1, D_MODEL)),
        _resident((ATT_GROUPS, HEAD_DIM)),
        _resident((ATT_GROUPS, HEAD_DIM)),
    ]
    n_w = 3 * ATT_GROUPS
    in_specs += [_layer_block((D_MODEL, ATT_GW), layer, COL_AQ // ATT_GW + k) for k in range(n_w)]
    out_specs, out_shape = [], []
    for d in ATT_DILATIONS:
        for _ in range(2):
            in_specs.append(pl.BlockSpec((d, PROJ_TM // d, HEAD_DIM), lambda bi, i: (0, i, 0)))
    for d in ATT_DILATIONS:
        for _ in range(3):
            out_specs.append(pl.BlockSpec((1, d, PROJ_TM // d, ATT_GW), lambda bi, i: (bi, 0, i, 0)))
            out_shape.append(jax.ShapeDtypeStruct((b, d, t // d, ATT_GW), BF16))
    return pl.pallas_call(
        _proj_b_kernel,
        grid=(b, nt),
        in_specs=in_specs,
        out_specs=out_specs,
        out_shape=out_shape,
        scratch_shapes=[pltpu.VMEM((2 * 3, ATT_HEADS, PROJ_TM, HEAD_DIM), F32)],
        compiler_params=pltpu.CompilerParams(
            dimension_semantics=("parallel", "parallel"), vmem_limit_bytes=VMEM_LIMIT_BYTES_V7X),
        name="proj_b",
    )(x3d, gain.reshape(1, D_MODEL), qn, kn, *([w_in] * n_w), *tables)


def _rope_tables(t):
    pos = np.arange(t, dtype=np.float64)
    inv = ROPE_THETA ** (-np.arange(0, HEAD_DIM, 2, dtype=np.float64) / HEAD_DIM)
    ang = pos[:, None] * inv[None, :]
    cos = np.concatenate([np.cos(ang), np.cos(ang)], axis=-1).astype(np.float32)
    sin = np.concatenate([-np.sin(ang), np.sin(ang)], axis=-1).astype(np.float32)
    tables = []
    for d in ATT_DILATIONS:
        for tab in (cos, sin):
            tables.append(jnp.asarray(np.ascontiguousarray(tab.reshape(t // d, d, HEAD_DIM).transpose(1, 0, 2))))
    return tables


def _hgrn_levels(c):
    levels = []
    m = 1
    while m < c:
        levels.append(m)
        m *= 2
    return levels


def _hgrn_kernel(layer, qa_ref, hf_ref, va_ref, gate_ref, lbl_ref, on_ref, o_ref, state_ref):
    c = HG_C

    @pl.when(pl.program_id(1) == 0)
    def _():
        state_ref[...] = jnp.zeros_like(state_ref)

    logits = lbl_ref[...]
    e = jnp.exp(logits - jnp.max(logits, axis=0, keepdims=True))
    sm = e / jnp.sum(e, axis=0, keepdims=True)
    lb_all = jnp.sum(sm[1:layer + 1], axis=0, keepdims=True) if layer > 0 else jnp.zeros((1, HG_W), F32)

    row = lax.broadcasted_iota(jnp.int32, (c, c), 0)
    colm = lax.broadcasted_iota(jnp.int32, (c, c), 1)
    tri = (colm <= row).astype(BF16)
    rix = lax.broadcasted_iota(jnp.int32, (c, HG_W), 0)
    sub = lax.broadcasted_iota(jnp.int32, (c // 8, 8, HG_W), 1)
    levels = _hgrn_levels(c)
    nrb = c // 8
    rblk = lambda a, i: a[i * 8:(i + 1) * 8]
    diag = [rblk(row == colm, i) for i in range(nrb)]
    sgns = {m: jnp.where(rix % (2 * m) >= m, 1.0, -1.0).astype(F32) for m in levels if 1 < m < 16}
    sels = {}
    for m in levels:
        sel = ((row // (2 * m)) == (colm // (2 * m))) & ((row % (2 * m)) >= m) & ((colm % (2 * m)) < m)
        sels[m] = [rblk(sel, i) for i in range(nrb)]
    odd_row = (rix % 2) == 1
    heads = [slice(hd * HEAD_DIM, (hd + 1) * HEAD_DIM) for hd in range(HG_HEADS)]
    nt_dims = (((1,), (1,)), ((), ()))

    def gates(ci):
        rows = pl.ds(ci * c, c)
        qb = qa_ref[rows, :]
        vb = va_ref[rows, :]
        f = lb_all + (1.0 - lb_all) * jax.nn.sigmoid(hf_ref[rows, :])
        k = 1.0 - f
        kb = k.astype(BF16)
        g = jnp.log2(f)
        g1 = g.astype(BF16)
        rem = g - g1.astype(F32)
        g2 = rem.astype(BF16)
        g3 = (rem - g2.astype(F32)).astype(BF16)
        gc = (jnp.dot(tri, g1, preferred_element_type=F32)
              + jnp.dot(tri, g2, preferred_element_type=F32)
              + jnp.dot(tri, g3, preferred_element_type=F32))
        g_last = gc[c - 1:c, :]
        qi = qb * jnp.exp2(gc).astype(BF16)
        ks = kb * jnp.exp2(g_last - gc).astype(BF16)
        decay = jnp.exp2(g_last)
        q1 = qb * jnp.where(odd_row, f, 1.0).astype(BF16)
        return rows, qb, vb, kb, gc, qi, ks, decay, q1

    def mix(rows, qb, vb, kb, gc, qi, ks, decay, q1):
        outs, accs = [], []
        for hd, cs in enumerate(heads):
            st = state_ref[hd]
            outs.append(lax.dot_general(qi[:, cs], st.astype(BF16), nt_dims, preferred_element_type=F32))
            upd = lax.dot_general(vb[:, cs], ks[:, cs], (((0,), (0,)), ((), ())), preferred_element_type=F32)
            state_ref[hd] = st * decay[:, cs] + upd
            p_d = lax.dot_general(jnp.concatenate([qb[:, cs], q1[:, cs]], axis=0), kb[:, cs], nt_dims,
                                  preferred_element_type=F32)
            accs.append([jnp.where(diag[i], rblk(p_d, i), jnp.where(sels[1][i], rblk(p_d, nrb + i), 0.0))
                         for i in range(nrb)])
        for m in levels[1:]:
            if m >= 16:
                kparts, qparts, qblocks = [], [], []
                for p in range(c // (2 * m)):
                    lo, mid, hi = p * 2 * m, p * 2 * m + m, (p + 1) * 2 * m
                    gb = gc[mid - 1:mid, :]
                    kparts.append(kb[lo:mid] * jnp.exp2(gb - gc[lo:mid]).astype(BF16))
                    kparts.append(kb[mid:hi])
                    qparts.append(qb[mid:hi] * jnp.exp2(gc[mid:hi] - gb).astype(BF16))
                    qblocks += list(range(mid // 8, hi // 8))
                kl = jnp.concatenate(kparts, axis=0)
                ql = jnp.concatenate(qparts, axis=0)
            else:
                qblocks = list(range(nrb))
                grp = max(8, 2 * m)
                g3d = gc.reshape(c // grp, grp, HG_W)
                bnd3 = None
                for p in range(grp // (2 * m)):
                    src = jnp.broadcast_to(g3d[:, p * 2 * m + m - 1:p * 2 * m + m, :], (c // grp, grp, HG_W))
                    bnd3 = src if bnd3 is None else jnp.where(sub // (2 * m) == p, src, bnd3)
                ex = jnp.exp2((gc - bnd3.reshape(c, HG_W)) * sgns[m]).astype(BF16)
                ql = qb * ex
                kl = kb * ex
            for hd, cs in enumerate(heads):
                p_l = lax.dot_general(ql[:, cs], kl[:, cs], nt_dims, preferred_element_type=F32)
                for n, i in enumerate(qblocks):
                    accs[hd][i] = jnp.where(sels[m][i], rblk(p_l, n), accs[hd][i])
        return rows, vb, outs, accs

    def emit(rows, vb, outs, accs):
        for hd, cs in enumerate(heads):
            a = jnp.concatenate(accs[hd], axis=0).astype(BF16)
            o = outs[hd] + jnp.dot(a, vb[:, cs], preferred_element_type=F32)
            o = _rms_rows(o) * on_ref[:, cs] * gate_ref[rows, cs].astype(F32)
            o_ref[rows, cs] = o.astype(BF16)

    n_chunks = HG_TB // c
    staged, pending = gates(0), None
    for ci in range(n_chunks):
        mixed = mix(*staged)
        staged = gates(ci + 1) if ci + 1 < n_chunks else None
        if pending is not None:
            emit(*pending)
        pending = mixed
    emit(*pending)


def _hgrn(layer, qa, hf, va, gate, lb_logits, out_norm, b, t):
    nt = t // HG_TB
    tile = lambda: pl.BlockSpec((HG_TB, HG_W), lambda bi, i: (bi * nt + i, 0))
    return pl.pallas_call(
        functools.partial(_hgrn_kernel, layer),
        grid=(b, nt),
        in_specs=[tile(), tile(), tile(), tile(), _resident((DEPTH, HG_W)), _resident((1, HG_W))],
        out_specs=tile(),
        out_shape=jax.ShapeDtypeStruct((b * t, HG_W), BF16),
        scratch_shapes=[pltpu.VMEM((HG_HEADS, HEAD_DIM, HEAD_DIM), F32)],
        compiler_params=pltpu.CompilerParams(
            dimension_semantics=("parallel", "arbitrary"), vmem_limit_bytes=VMEM_LIMIT_BYTES_V7X),
        name="hgrn",
    )(qa, hf, va, gate, lb_logits, out_norm.reshape(1, HG_W))


def _attn_kernel(*refs):
    ins, o_ref, osc_ref, lsc_ref = refs[:15], refs[15], refs[16], refs[17]
    qi = lax.broadcasted_iota(jnp.int32, (ATT_BACK, 2 * ATT_BACK), 0)
    ki = lax.broadcasted_iota(jnp.int32, (ATT_BACK, 2 * ATT_BACK), 1)
    band = (ki >= qi) & (ki <= qi + ATT_BACK)
    first_key = jnp.where(pl.program_id(2) == 0, ATT_BACK, 0)
    band_first = band & (ki >= first_key)
    for g, d in enumerate(ATT_DILATIONS):
        q_ref, kc_ref, kp_ref, vc_ref, vp_ref = ins[5 * g:5 * g + 5]
        nblk = ATT_TT // d // ATT_BACK
        for r in range(d):
            for j in range(nblk):
                q = q_ref[r, j * ATT_BACK:(j + 1) * ATT_BACK, :]
                if j == 0:
                    kk = jnp.concatenate([kp_ref[r], kc_ref[r, 0:ATT_BACK, :]], axis=0)
                    vv = jnp.concatenate([vp_ref[r], vc_ref[r, 0:ATT_BACK, :]], axis=0)
                    mask = band_first
                else:
                    kk = kc_ref[r, (j - 1) * ATT_BACK:(j + 1) * ATT_BACK, :]
                    vv = vc_ref[r, (j - 1) * ATT_BACK:(j + 1) * ATT_BACK, :]
                    mask = band
                s = lax.dot_general(q, kk, (((1,), (1,)), ((), ())), preferred_element_type=F32)
                s = jnp.where(mask, s, -jnp.inf)
                mx = jnp.max(s, axis=-1, keepdims=True)
                p = jnp.exp2(s - mx)
                den = jnp.sum(p, axis=-1, keepdims=True)
                o = jnp.dot(p.astype(BF16), vv, preferred_element_type=F32) / den
                lse = mx + jnp.log2(den)
                dst = pl.ds(j * ATT_BACK * d + r, ATT_BACK, stride=d) if d > 1 else pl.ds(j * ATT_BACK, ATT_BACK)
                osc_ref[g, dst, :] = o
                lsc_ref[g, dst, :] = jnp.broadcast_to(lse, (ATT_BACK, HEAD_DIM))
    for j in range(ATT_TT // ATT_BACK):
        rs = slice(j * ATT_BACK, (j + 1) * ATT_BACK)
        l0, l1, l2 = lsc_ref[0, rs, :], lsc_ref[1, rs, :], lsc_ref[2, rs, :]
        mx = jnp.maximum(jnp.maximum(l0, l1), l2)
        w0, w1, w2 = jnp.exp2(l0 - mx), jnp.exp2(l1 - mx), jnp.exp2(l2 - mx)
        o = (w0 * osc_ref[0, rs, :] + w1 * osc_ref[1, rs, :] + w2 * osc_ref[2, rs, :]) / (w0 + w1 + w2)
        o_ref[rs, :] = o.astype(BF16)


def _attn(qkv, b, t):
    nt = t // ATT_TT
    in_specs, args = [], []
    for g, d in enumerate(ATT_DILATIONS):
        rows = ATT_TT // d
        nb = rows // ATT_BACK
        cur = lambda rows=rows, d=d: pl.BlockSpec((None, d, rows, HEAD_DIM), lambda bi, h, i: (bi, 0, i, h))
        prev = lambda nb=nb, d=d: pl.BlockSpec(
            (None, d, ATT_BACK, HEAD_DIM), lambda bi, h, i: (bi, 0, jnp.maximum(i * nb - 1, 0), h))
        q, k, v = qkv[g]
        in_specs += [cur(), cur(), prev(), cur(), prev()]
        args += [q, k, k, v, v]
    return pl.pallas_call(
        _attn_kernel,
        grid=(b, ATT_HEADS, nt),
        in_specs=in_specs,
        out_specs=pl.BlockSpec((None, ATT_TT, HEAD_DIM), lambda bi, h, i: (bi, i, h)),
        out_shape=jax.ShapeDtypeStruct((b, t, ATT_GW), BF16),
        scratch_shapes=[pltpu.VMEM((ATT_GROUPS, ATT_TT, HEAD_DIM), F32),
                        pltpu.VMEM((ATT_GROUPS, ATT_TT, HEAD_DIM), F32)],
        compiler_params=pltpu.CompilerParams(
            dimension_semantics=("parallel", "parallel", "arbitrary"), vmem_limit_bytes=VMEM_LIMIT_BYTES_V7X),
        name="attn",
    )(*args)


def _merge_kernel(x_ref, oa_ref, ob_ref, sga_ref, sgb_ref, wa_ref, wb_ref, wo_ref, o_ref):
    ya = jnp.dot(oa_ref[...], wa_ref[...], preferred_element_type=F32)
    yb = jnp.dot(ob_ref[...], wb_ref[...], preferred_element_type=F32)
    merged = sga_ref[...].astype(F32) * ya + sgb_ref[...].astype(F32) * yb
    o_ref[...] = x_ref[...] + jnp.dot(merged.astype(BF16), wo_ref[...], preferred_element_type=F32)


def _merge(x2d, oa, ob, sga, sgb, wa, wb, wo, layer):
    n = x2d.shape[0]
    tile = lambda w: pl.BlockSpec((MERGE_TM, w), lambda i: (i, 0))
    return pl.pallas_call(
        _merge_kernel,
        grid=(n // MERGE_TM,),
        in_specs=[tile(D_MODEL), tile(HG_W), tile(ATT_GW), tile(D_MODEL), tile(D_MODEL),
                  _layer_block((HG_W, D_MODEL), layer), _layer_block((ATT_GW, D_MODEL), layer),
                  _layer_block((D_MODEL, D_MODEL), layer)],
        out_specs=tile(D_MODEL),
        out_shape=jax.ShapeDtypeStruct((n, D_MODEL), F32),
        compiler_params=pltpu.CompilerParams(
            dimension_semantics=("parallel",), vmem_limit_bytes=VMEM_LIMIT_BYTES_V7X),
        name="merge",
    )(x2d, oa, ob, sga, sgb, wa, wb, wo)


def kernel(x, ffn1_norm, ffn1_w_in, ffn1_w_out, mix_norm, w_in, hgrn_lb_logits, hgrn_out_norm, attn_q_norm,
           attn_k_norm, w_branch_a, w_branch_b, w_out, ffn2_norm, ffn2_w_in, ffn2_w_out):
    b, t, _ = x.shape
    assert x.shape[2] == D_MODEL and t % ATT_TT == 0 and t % HG_TB == 0
    assert COL_GA % PROJ_NC == 0 and COL_AQ % ATT_GW == 0
    tables = _rope_tables(t)
    ffn1_w_in, ffn1_w_out, w_in, w_branch_a, w_branch_b, w_out, ffn2_w_in, ffn2_w_out = (
        w.astype(BF16) for w in (ffn1_w_in, ffn1_w_out, w_in, w_branch_a, w_branch_b, w_out, ffn2_w_in, ffn2_w_out))
    x2d = x.reshape(b * t, D_MODEL)
    for l in range(DEPTH):
        x2d = _ffn(x2d, ffn1_norm[l], ffn1_w_in, ffn1_w_out, l)

        qa, hf, va, gate, sga, sgb = _proj_a(x2d, mix_norm[l], w_in, l)
        qkv = _proj_b(x2d.reshape(b, t, D_MODEL), mix_norm[l], w_in, l, attn_q_norm[l], attn_k_norm[l], tables)
        oa = _hgrn(l, qa, hf, va, gate, hgrn_lb_logits, hgrn_out_norm[l], b, t)
        ob = _attn([qkv[3 * g:3 * g + 3] for g in range(ATT_GROUPS)], b, t)
        x2d = _merge(x2d, oa, ob.reshape(b * t, ATT_GW), sga, sgb, w_branch_a, w_branch_b, w_out, l)

        x2d = _ffn(x2d, ffn2_norm[l], ffn2_w_in, ffn2_w_out, l)
    return x2d.reshape(b, t, D_MODEL)
```

```python
import functools
import math

import jax
import jax.numpy as jnp
import numpy as np
from jax import lax
from jax.experimental import pallas as pl
from jax.experimental.pallas import tpu as pltpu

F32 = jnp.float32
BF16 = jnp.bfloat16

D_MODEL = 1024
DEPTH = 2
D_FF = 2816
HEAD_DIM = 128
HG_HEADS = D_MODEL // HEAD_DIM
HG_W = HG_HEADS * HEAD_DIM
ATT_DILATIONS = (1, 4, 16)
ATT_BACK = 128
ATT_GROUPS = 3
ATT_HEADS = 4
ATT_GW = ATT_HEADS * HEAD_DIM
ATT_W = ATT_GROUPS * ATT_GW
ROPE_THETA = 10000.0
EPS = 1e-6

COL_HQ, COL_HF, COL_HI, COL_HG = 0, HG_W, 2 * HG_W, 3 * HG_W
COL_AQ = 4 * HG_W
COL_AK = COL_AQ + ATT_W
COL_AV = COL_AK + ATT_W
COL_GA = COL_AV + ATT_W
COL_GB = COL_GA + D_MODEL
P_IN = COL_GB + D_MODEL

VMEM_LIMIT_BYTES_V7X = 56 * 1024 * 1024

FFN_TM = 512
FFN_FC = 256
PROJ_A_TM = 1024
PROJ_TM = 512
PROJ_NC = 512
HG_TB = 1024
HG_C = 128
ATT_TT = ATT_BACK * ATT_DILATIONS[-1]
MERGE_TM = 1024


def _resident(shape):
    nd = len(shape)
    return pl.BlockSpec(shape, lambda *_: (0,) * nd, pipeline_mode=pl.Buffered(1))


def _layer_block(shape, layer, col_block=0):
    return pl.BlockSpec((None,) + tuple(shape), lambda *_: (layer, 0, col_block), pipeline_mode=pl.Buffered(1))


def _rms_rows(x):
    return x * lax.rsqrt(jnp.mean(x * x, axis=-1, keepdims=True) + EPS)


def _silu(x):
    return x * jax.nn.sigmoid(x)


def _ffn_kernel(x_ref, g_ref, win_ref, wout_ref, o_ref, act_ref):
    x = x_ref[...]
    h = (_rms_rows(x) * g_ref[...]).astype(BF16)
    for c in range(D_FF // FFN_FC):
        lo = c * FFN_FC
        a = jnp.dot(h, win_ref[:, lo:lo + FFN_FC], preferred_element_type=F32)
        b = jnp.dot(h, win_ref[:, D_FF + lo:D_FF + lo + FFN_FC], preferred_element_type=F32)
        act_ref[:, lo:lo + FFN_FC] = (_silu(a) * b).astype(BF16)
    y = jnp.dot(act_ref[...], wout_ref[...], preferred_element_type=F32)
    o_ref[...] = x + 0.5 * y


def _ffn(x2d, gain, w_in, w_out, layer):
    n = x2d.shape[0]
    return pl.pallas_call(
        _ffn_kernel,
        grid=(n // FFN_TM,),
        in_specs=[
            pl.BlockSpec((FFN_TM, D_MODEL), lambda i: (i, 0)),
            _resident((1, D_MODEL)),
            _layer_block((D_MODEL, 2 * D_FF), layer),
            _layer_block((D_FF, D_MODEL), layer),
        ],
        out_specs=pl.BlockSpec((FFN_TM, D_MODEL), lambda i: (i, 0)),
        out_shape=jax.ShapeDtypeStruct((n, D_MODEL), F32),
        scratch_shapes=[pltpu.VMEM((FFN_TM, D_FF), BF16)],
        compiler_params=pltpu.CompilerParams(
            dimension_semantics=("parallel",), vmem_limit_bytes=VMEM_LIMIT_BYTES_V7X),
        name="ffn",
    )(x2d, gain.reshape(1, D_MODEL), w_in, w_out)


def _proj_a_kernel(x_ref, g_ref, wh_ref, *refs):
    n_gate = 2 * D_MODEL // PROJ_NC
    wg_refs, (qa_ref, hf_ref, va_ref, gate_ref, sga_ref, sgb_ref) = refs[:n_gate], refs[n_gate:]
    h = (_rms_rows(x_ref[...]) * g_ref[...]).astype(BF16)
    outs = (
        (qa_ref, _silu), (hf_ref, None), (va_ref, None), (gate_ref, _silu),
        (sga_ref, jax.nn.sigmoid), (sgb_ref, jax.nn.sigmoid))
    for s, (o_ref, fn) in enumerate(outs):
        for c in range(HG_W // PROJ_NC):
            lo = c * PROJ_NC
            if s < 4:
                w = wh_ref[:, s * HG_W + lo:s * HG_W + lo + PROJ_NC]
            else:
                w = wg_refs[(s - 4) * (D_MODEL // PROJ_NC) + c][...]
            y = jnp.dot(h, w, preferred_element_type=F32)
            if fn is not None:
                y = fn(y)
            o_ref[:, lo:lo + PROJ_NC] = y.astype(o_ref.dtype)


def _proj_a(x2d, gain, w_in, layer):
    n = x2d.shape[0]
    tile = lambda: pl.BlockSpec((PROJ_A_TM, HG_W), lambda i: (i, 0))
    bf = jax.ShapeDtypeStruct((n, HG_W), BF16)
    gate_blocks = [_layer_block((D_MODEL, PROJ_NC), layer, COL_GA // PROJ_NC + c)
                   for c in range(2 * D_MODEL // PROJ_NC)]
    return pl.pallas_call(
        _proj_a_kernel,
        grid=(n // PROJ_A_TM,),
        in_specs=[tile(), _resident((1, D_MODEL)), _layer_block((D_MODEL, 4 * HG_W), layer)] + gate_blocks,
        out_specs=[tile() for _ in range(6)],
        out_shape=[bf, jax.ShapeDtypeStruct((n, HG_W), F32), bf, bf, bf, bf],
        compiler_params=pltpu.CompilerParams(
            dimension_semantics=("parallel",), vmem_limit_bytes=VMEM_LIMIT_BYTES_V7X),
        name="proj_a",
    )(x2d, gain.reshape(1, D_MODEL), w_in, *([w_in] * len(gate_blocks)))


def _rope(y, cos, sin_signed):
    return y * cos + pltpu.roll(y, HEAD_DIM // 2, axis=1) * sin_signed


def _proj_b_kernel(x_ref, g_ref, qn_ref, kn_ref, *refs):
    w_refs, tab_refs, out_refs, slab_ref = refs[:9], refs[9:15], refs[15:24], refs[24]
    h = (_rms_rows(x_ref[...]) * g_ref[...]).astype(BF16)
    tabs = [tab_refs[2 * g:2 * g + 2] for g in range(ATT_GROUPS)]
    outs = [out_refs[3 * g:3 * g + 3] for g in range(ATT_GROUPS)]
    scale = HEAD_DIM ** -0.5 * math.log2(math.e)
    order = [(s, g) for g in range(ATT_GROUPS) for s in (0, 1, 2)]
    for s, g in order:
        d = ATT_DILATIONS[g]
        cos_ref, sin_ref = tabs[g]
        rows = PROJ_TM // d
        if True:
            o_ref = outs[g][s]
            y = jnp.dot(h, w_refs[s * ATT_GROUPS + g][...], preferred_element_type=F32)
            for hd in range(ATT_HEADS):
                cs = slice(hd * HEAD_DIM, (hd + 1) * HEAD_DIM)
                if d > 1:
                    slab = slab_ref.at[(g - 1) * 3 + s, hd]
                    slab[...] = y[:, cs]
                    yh = jnp.concatenate([slab[pl.ds(r, rows, stride=d), :] for r in range(d)], axis=0)
                else:
                    yh = y[:, cs]
                if s < 2:
                    gain = (qn_ref if s == 0 else kn_ref)[g:g + 1, :]
                    yh = _rope(_rms_rows(yh) * gain, cos_ref[...].reshape(PROJ_TM, HEAD_DIM),
                               sin_ref[...].reshape(PROJ_TM, HEAD_DIM))
                    if s == 0:
                        yh = yh * scale
                o_ref[0, :, :, cs] = yh.astype(BF16).reshape(d, rows, HEAD_DIM)


def _proj_b(x3d, gain, w_in, layer, qn, kn, tables):
    b, t, _ = x3d.shape
    nt = t // PROJ_TM
    in_specs = [
        pl.BlockSpec((None, PROJ_TM, D_MODEL), lambda bi, i: (bi, i, 0)),
        _resident((1, D_MODEL)),
        _resident((ATT_GROUPS, HEAD_DIM)),
        _resident((ATT_GROUPS, HEAD_DIM)),
    ]
    n_w = 3 * ATT_GROUPS
    in_specs += [_layer_block((D_MODEL, ATT_GW), layer, COL_AQ // ATT_GW + k) for k in range(n_w)]
    out_specs, out_shape = [], []
    for d in ATT_DILATIONS:
        for _ in range(2):
            in_specs.append(pl.BlockSpec((d, PROJ_TM // d, HEAD_DIM), lambda bi, i: (0, i, 0)))
    for d in ATT_DILATIONS:
        for _ in range(3):
            out_specs.append(pl.BlockSpec((1, d, PROJ_TM // d, ATT_GW), lambda bi, i: (bi, 0, i, 0)))
            out_shape.append(jax.ShapeDtypeStruct((b, d, t // d, ATT_GW), BF16))
    return pl.pallas_call(
        _proj_b_kernel,
        grid=(b, nt),
        in_specs=in_specs,
        out_specs=out_specs,
        out_shape=out_shape,
        scratch_shapes=[pltpu.VMEM((2 * 3, ATT_HEADS, PROJ_TM, HEAD_DIM), F32)],
        compiler_params=pltpu.CompilerParams(
            dimension_semantics=("parallel", "parallel"), vmem_limit_bytes=VMEM_LIMIT_BYTES_V7X),
        name="proj_b",
    )(x3d, gain.reshape(1, D_MODEL), qn, kn, *([w_in] * n_w), *tables)


def _rope_tables(t):
    pos = np.arange(t, dtype=np.float64)
    inv = ROPE_THETA ** (-np.arange(0, HEAD_DIM, 2, dtype=np.float64) / HEAD_DIM)
    ang = pos[:, None] * inv[None, :]
    cos = np.concatenate([np.cos(ang), np.cos(ang)], axis=-1).astype(np.float32)
    sin = np.concatenate([-np.sin(ang), np.sin(ang)], axis=-1).astype(np.float32)
    tables = []
    for d in ATT_DILATIONS:
        for tab in (cos, sin):
            tables.append(jnp.asarray(np.ascontiguousarray(tab.reshape(t // d, d, HEAD_DIM).transpose(1, 0, 2))))
    return tables


def _hgrn_levels(c):
    levels = []
    m = 1
    while m < c:
        levels.append(m)
        m *= 2
    return levels


def _hgrn_kernel(layer, qa_ref, hf_ref, va_ref, gate_ref, lbl_ref, on_ref, o_ref, state_ref):
    c = HG_C

    @pl.when(pl.program_id(1) == 0)
    def _():
        state_ref[...] = jnp.zeros_like(state_ref)

    logits = lbl_ref[...]
    e = jnp.exp(logits - jnp.max(logits, axis=0, keepdims=True))
    sm = e / jnp.sum(e, axis=0, keepdims=True)
    lb_all = jnp.sum(sm[1:layer + 1], axis=0, keepdims=True) if layer > 0 else jnp.zeros((1, HG_W), F32)

    row = lax.broadcasted_iota(jnp.int32, (c, c), 0)
    colm = lax.broadcasted_iota(jnp.int32, (c, c), 1)
    tri = (colm <= row).astype(BF16)
    rix = lax.broadcasted_iota(jnp.int32, (c, HG_W), 0)
    sub = lax.broadcasted_iota(jnp.int32, (c // 8, 8, HG_W), 1)
    levels = _hgrn_levels(c)
    nrb = c // 8
    rblk = lambda a, i: a[i * 8:(i + 1) * 8]
    diag = [rblk(row == colm, i) for i in range(nrb)]
    sgns = {m: jnp.where(rix % (2 * m) >= m, 1.0, -1.0).astype(F32) for m in levels if 1 < m < 16}
    sels = {}
    for m in levels:
        sel = ((row // (2 * m)) == (colm // (2 * m))) & ((row % (2 * m)) >= m) & ((colm % (2 * m)) < m)
        sels[m] = [rblk(sel, i) for i in range(nrb)]
    odd_row = (rix % 2) == 1
    heads = [slice(hd * HEAD_DIM, (hd + 1) * HEAD_DIM) for hd in range(HG_HEADS)]
    nt_dims = (((1,), (1,)), ((), ()))

    def gates(ci):
        rows = pl.ds(ci * c, c)
        qb = qa_ref[rows, :]
        vb = va_ref[rows, :]
        f = lb_all + (1.0 - lb_all) * jax.nn.sigmoid(hf_ref[rows, :])
        k = 1.0 - f
        kb = k.astype(BF16)
        g = jnp.log2(f)
        g1 = g.astype(BF16)
        rem = g - g1.astype(F32)
        g2 = rem.astype(BF16)
        g3 = (rem - g2.astype(F32)).astype(BF16)
        gc = (jnp.dot(tri, g1, preferred_element_type=F32)
              + jnp.dot(tri, g2, preferred_element_type=F32)
              + jnp.dot(tri, g3, preferred_element_type=F32))
        g_last = gc[c - 1:c, :]
        qi = qb * jnp.exp2(gc).astype(BF16)
        ks = kb * jnp.exp2(g_last - gc).astype(BF16)
        decay = jnp.exp2(g_last)
        q1 = qb * jnp.where(odd_row, f, 1.0).astype(BF16)
        return rows, qb, vb, kb, gc, qi, ks, decay, q1

    def mix(rows, qb, vb, kb, gc, qi, ks, decay, q1):
        outs, accs = [], []
        for hd, cs in enumerate(heads):
            st = state_ref[hd]
            outs.append(lax.dot_general(qi[:, cs], st.astype(BF16), nt_dims, preferred_element_type=F32))
            upd = lax.dot_general(vb[:, cs], ks[:, cs], (((0,), (0,)), ((), ())), preferred_element_type=F32)
            state_ref[hd] = st * decay[:, cs] + upd
            p_d = lax.dot_general(jnp.concatenate([qb[:, cs], q1[:, cs]], axis=0), kb[:, cs], nt_dims,
                                  preferred_element_type=F32)
            accs.append([jnp.where(diag[i], rblk(p_d, i), jnp.where(sels[1][i], rblk(p_d, nrb + i), 0.0))
                         for i in range(nrb)])
        for m in levels[1:]:
            if m >= 16:
                kparts, qparts, qblocks = [], [], []
                for p in range(c // (2 * m)):
                    lo, mid, hi = p * 2 * m, p * 2 * m + m, (p + 1) * 2 * m
                    gb = gc[mid - 1:mid, :]
                    kparts.append(kb[lo:mid] * jnp.exp2(gb - gc[lo:mid]).astype(BF16))
                    kparts.append(kb[mid:hi])
                    qparts.append(qb[mid:hi] * jnp.exp2(gc[mid:hi] - gb).astype(BF16))
                    qblocks += list(range(mid // 8, hi // 8))
                kl = jnp.concatenate(kparts[:-1], axis=0)
                ql = jnp.concatenate(qparts, axis=0)
            else:
                qblocks = list(range(nrb))
                grp = max(8, 2 * m)
                g3d = gc.reshape(c // grp, grp, HG_W)
                bnd3 = None
                for p in range(grp // (2 * m)):
                    src = jnp.broadcast_to(g3d[:, p * 2 * m + m - 1:p * 2 * m + m, :], (c // grp, grp, HG_W))
                    bnd3 = src if bnd3 is None else jnp.where(sub // (2 * m) == p, src, bnd3)
                ex = jnp.exp2((gc - bnd3.reshape(c, HG_W)) * sgns[m]).astype(BF16)
                ql = qb * ex
                kl = kb * ex
            for hd, cs in enumerate(heads):
                p_l = lax.dot_general(ql[:, cs], kl[:, cs], nt_dims, preferred_element_type=F32)
                n_keys = kl.shape[0]
                for n, i in enumerate(qblocks):
                    blk = rblk(p_l, n)
                    if n_keys < c:
                        blk = jnp.concatenate([blk, jnp.zeros((8, c - n_keys), F32)], axis=1)
                    accs[hd][i] = jnp.where(sels[m][i], blk, accs[hd][i])
        return rows, vb, outs, accs

    def emit(rows, vb, outs, accs):
        for hd, cs in enumerate(heads):
            a = jnp.concatenate(accs[hd], axis=0).astype(BF16)
            o = outs[hd] + jnp.dot(a, vb[:, cs], preferred_element_type=F32)
            o = _rms_rows(o) * on_ref[:, cs] * gate_ref[rows, cs].astype(F32)
            o_ref[rows, cs] = o.astype(BF16)

    n_chunks = HG_TB // c
    staged, pending = gates(0), None
    for ci in range(n_chunks):
        mixed = mix(*staged)
        staged = gates(ci + 1) if ci + 1 < n_chunks else None
        if pending is not None:
            emit(*pending)
        pending = mixed
    emit(*pending)


def _hgrn(layer, qa, hf, va, gate, lb_logits, out_norm, b, t):
    nt = t // HG_TB
    tile = lambda: pl.BlockSpec((HG_TB, HG_W), lambda bi, i: (bi * nt + i, 0))
    return pl.pallas_call(
        functools.partial(_hgrn_kernel, layer),
        grid=(b, nt),
        in_specs=[tile(), tile(), tile(), tile(), _resident((DEPTH, HG_W)), _resident((1, HG_W))],
        out_specs=tile(),
        out_shape=jax.ShapeDtypeStruct((b * t, HG_W), BF16),
        scratch_shapes=[pltpu.VMEM((HG_HEADS, HEAD_DIM, HEAD_DIM), F32)],
        compiler_params=pltpu.CompilerParams(
            dimension_semantics=("parallel", "arbitrary"), vmem_limit_bytes=VMEM_LIMIT_BYTES_V7X),
        name="hgrn",
    )(qa, hf, va, gate, lb_logits, out_norm.reshape(1, HG_W))


def _attn_kernel(*refs):
    ins, o_ref, osc_ref, lsc_ref = refs[:15], refs[15], refs[16], refs[17]
    qi = lax.broadcasted_iota(jnp.int32, (ATT_BACK, 2 * ATT_BACK), 0)
    ki = lax.broadcasted_iota(jnp.int32, (ATT_BACK, 2 * ATT_BACK), 1)
    band = (ki >= qi) & (ki <= qi + ATT_BACK)
    first_key = jnp.where(pl.program_id(2) == 0, ATT_BACK, 0)
    band_first = band & (ki >= first_key)
    for g, d in enumerate(ATT_DILATIONS):
        q_ref, kc_ref, kp_ref, vc_ref, vp_ref = ins[5 * g:5 * g + 5]
        nblk = ATT_TT // d // ATT_BACK
        for r in range(d):
            for j in range(nblk):
                q = q_ref[r, j * ATT_BACK:(j + 1) * ATT_BACK, :]
                if j == 0:
                    kk = jnp.concatenate([kp_ref[r], kc_ref[r, 0:ATT_BACK, :]], axis=0)
                    vv = jnp.concatenate([vp_ref[r], vc_ref[r, 0:ATT_BACK, :]], axis=0)
                    mask = band_first
                else:
                    kk = kc_ref[r, (j - 1) * ATT_BACK:(j + 1) * ATT_BACK, :]
                    vv = vc_ref[r, (j - 1) * ATT_BACK:(j + 1) * ATT_BACK, :]
                    mask = band
                s = lax.dot_general(q, kk, (((1,), (1,)), ((), ())), preferred_element_type=F32)
                s = jnp.where(mask, s, -jnp.inf)
                mx = jnp.max(s, axis=-1, keepdims=True)
                p = jnp.exp2(s - mx)
                den = jnp.sum(p, axis=-1, keepdims=True)
                o = jnp.dot(p.astype(BF16), vv, preferred_element_type=F32) / den
                lse = mx + jnp.log2(den)
                dst = pl.ds(j * ATT_BACK * d + r, ATT_BACK, stride=d) if d > 1 else pl.ds(j * ATT_BACK, ATT_BACK)
                osc_ref[g, dst, :] = o
                lsc_ref[g, dst, :] = jnp.broadcast_to(lse, (ATT_BACK, HEAD_DIM))
    for j in range(ATT_TT // ATT_BACK):
        rs = slice(j * ATT_BACK, (j + 1) * ATT_BACK)
        l0, l1, l2 = lsc_ref[0, rs, :], lsc_ref[1, rs, :], lsc_ref[2, rs, :]
        mx = jnp.maximum(jnp.maximum(l0, l1), l2)
        w0, w1, w2 = jnp.exp2(l0 - mx), jnp.exp2(l1 - mx), jnp.exp2(l2 - mx)
        o = (w0 * osc_ref[0, rs, :] + w1 * osc_ref[1, rs, :] + w2 * osc_ref[2, rs, :]) / (w0 + w1 + w2)
        o_ref[rs, :] = o.astype(BF16)


def _attn(qkv, b, t):
    nt = t // ATT_TT
    in_specs, args = [], []
    for g, d in enumerate(ATT_DILATIONS):
        rows = ATT_TT // d
        nb = rows // ATT_BACK
        cur = lambda rows=rows, d=d: pl.BlockSpec((None, d, rows, HEAD_DIM), lambda bi, h, i: (bi, 0, i, h))
        prev = lambda nb=nb, d=d: pl.BlockSpec(
            (None, d, ATT_BACK, HEAD_DIM), lambda bi, h, i: (bi, 0, jnp.maximum(i * nb - 1, 0), h))
        q, k, v = qkv[g]
        in_specs += [cur(), cur(), prev(), cur(), prev()]
        args += [q, k, k, v, v]
    return pl.pallas_call(
        _attn_kernel,
        grid=(b, ATT_HEADS, nt),
        in_specs=in_specs,
        out_specs=pl.BlockSpec((None, ATT_TT, HEAD_DIM), lambda bi, h, i: (bi, i, h)),
        out_shape=jax.ShapeDtypeStruct((b, t, ATT_GW), BF16),
        scratch_shapes=[pltpu.VMEM((ATT_GROUPS, ATT_TT, HEAD_DIM), F32),
                        pltpu.VMEM((ATT_GROUPS, ATT_TT, HEAD_DIM), F32)],
        compiler_params=pltpu.CompilerParams(
            dimension_semantics=("parallel", "parallel", "arbitrary"), vmem_limit_bytes=VMEM_LIMIT_BYTES_V7X),
        name="attn",
    )(*args)


def _merge_kernel(x_ref, oa_ref, ob_ref, sga_ref, sgb_ref, wa_ref, wb_ref, wo_ref, o_ref):
    ya = jnp.dot(oa_ref[...], wa_ref[...], preferred_element_type=F32)
    yb = jnp.dot(ob_ref[...], wb_ref[...], preferred_element_type=F32)
    merged = sga_ref[...].astype(F32) * ya + sgb_ref[...].astype(F32) * yb
    o_ref[...] = x_ref[...] + jnp.dot(merged.astype(BF16), wo_ref[...], preferred_element_type=F32)


def _merge(x2d, oa, ob, sga, sgb, wa, wb, wo, layer):
    n = x2d.shape[0]
    tile = lambda w: pl.BlockSpec((MERGE_TM, w), lambda i: (i, 0))
    return pl.pallas_call(
        _merge_kernel,
        grid=(n // MERGE_TM,),
        in_specs=[tile(D_MODEL), tile(HG_W), tile(ATT_GW), tile(D_MODEL), tile(D_MODEL),
                  _layer_block((HG_W, D_MODEL), layer), _layer_block((ATT_GW, D_MODEL), layer),
                  _layer_block((D_MODEL, D_MODEL), layer)],
        out_specs=tile(D_MODEL),
        out_shape=jax.ShapeDtypeStruct((n, D_MODEL), F32),
        compiler_params=pltpu.CompilerParams(
            dimension_semantics=("parallel",), vmem_limit_bytes=VMEM_LIMIT_BYTES_V7X),
        name="merge",
    )(x2d, oa, ob, sga, sgb, wa, wb, wo)


def kernel(x, ffn1_norm, ffn1_w_in, ffn1_w_out, mix_norm, w_in, hgrn_lb_logits, hgrn_out_norm, attn_q_norm,
           attn_k_norm, w_branch_a, w_branch_b, w_out, ffn2_norm, ffn2_w_in, ffn2_w_out):
    b, t, _ = x.shape
    assert x.shape[2] == D_MODEL and t % ATT_TT == 0 and t % HG_TB == 0
    assert COL_GA % PROJ_NC == 0 and COL_AQ % ATT_GW == 0
    tables = _rope_tables(t)
    ffn1_w_in, ffn1_w_out, w_in, w_branch_a, w_branch_b, w_out, ffn2_w_in, ffn2_w_out = (
        w.astype(BF16) for w in (ffn1_w_in, ffn1_w_out, w_in, w_branch_a, w_branch_b, w_out, ffn2_w_in, ffn2_w_out))
    x2d = x.reshape(b * t, D_MODEL)
    for l in range(DEPTH):
        x2d = _ffn(x2d, ffn1_norm[l], ffn1_w_in, ffn1_w_out, l)

        qa, hf, va, gate, sga, sgb = _proj_a(x2d, mix_norm[l], w_in, l)
        qkv = _proj_b(x2d.reshape(b, t, D_MODEL), mix_norm[l], w_in, l, attn_q_norm[l], attn_k_norm[l], tables)
        oa = _hgrn(l, qa, hf, va, gate, hgrn_lb_logits, hgrn_out_norm[l], b, t)
        ob = _attn([qkv[3 * g:3 * g + 3] for g in range(ATT_GROUPS)], b, t)
        x2d = _merge(x2d, oa, ob.reshape(b * t, ATT_GW), sga, sgb, w_branch_a, w_branch_b, w_out, l)

        x2d = _ffn(x2d, ffn2_norm[l], ffn2_w_in, ffn2_w_out, l)
    return x2d.reshape(b, t, D_MODEL)
```

```python
import functools
import math

import jax
import jax.numpy as jnp
import numpy as np
from jax import lax
from jax.experimental import pallas as pl
from jax.experimental.pallas import tpu as pltpu

F32 = jnp.float32
BF16 = jnp.bfloat16

D_MODEL = 1024
DEPTH = 2
D_FF = 2816
HEAD_DIM = 128
HG_HEADS = D_MODEL // HEAD_DIM
HG_W = HG_HEADS * HEAD_DIM
ATT_DILATIONS = (1, 4, 16)
ATT_BACK = 128
ATT_GROUPS = 3
ATT_HEADS = 4
ATT_GW = ATT_HEADS * HEAD_DIM
ATT_W = ATT_GROUPS * ATT_GW
ROPE_THETA = 10000.0
EPS = 1e-6

COL_HQ, COL_HF, COL_HI, COL_HG = 0, HG_W, 2 * HG_W, 3 * HG_W
COL_AQ = 4 * HG_W
COL_AK = COL_AQ + ATT_W
COL_AV = COL_AK + ATT_W
COL_GA = COL_AV + ATT_W
COL_GB = COL_GA + D_MODEL
P_IN = COL_GB + D_MODEL

VMEM_LIMIT_BYTES_V7X = 56 * 1024 * 1024
SUBLANES = 8
BF16_ROWS = 16

FFN_TM = 512
FFN_FC = 256
PROJ_TM = 256
PROJ_NC = 512
HG_TB = 1024
HG_C = 128
ATT_TT = ATT_BACK * ATT_DILATIONS[-1]
MERGE_TM = 1024


def _resident(shape):
    nd = len(shape)
    return pl.BlockSpec(shape, lambda *_: (0,) * nd, pipeline_mode=pl.Buffered(1))


def _layer_block(shape, layer, col_block=0):
    return pl.BlockSpec((None,) + tuple(shape), lambda *_: (layer, 0, col_block), pipeline_mode=pl.Buffered(1))


def _rms_rows(x):
    return x * lax.rsqrt(jnp.mean(x * x, axis=-1, keepdims=True) + EPS)


def _silu(x):
    return x * jax.nn.sigmoid(x)


def _ffn_kernel(x_ref, g_ref, win_ref, wout_ref, o_ref, act_ref):
    x = x_ref[...]
    h = (_rms_rows(x) * g_ref[...]).astype(BF16)
    for c in range(D_FF // FFN_FC):
        lo = c * FFN_FC
        a = jnp.dot(h, win_ref[:, lo:lo + FFN_FC], preferred_element_type=F32)
        b = jnp.dot(h, win_ref[:, D_FF + lo:D_FF + lo + FFN_FC], preferred_element_type=F32)
        act_ref[:, lo:lo + FFN_FC] = (_silu(a) * b).astype(BF16)
    y = jnp.dot(act_ref[...], wout_ref[...], preferred_element_type=F32)
    o_ref[...] = x + 0.5 * y


def _ffn(x2d, gain, w_in, w_out, layer):
    n = x2d.shape[0]
    return pl.pallas_call(
        _ffn_kernel,
        grid=(n // FFN_TM,),
        in_specs=[
            pl.BlockSpec((FFN_TM, D_MODEL), lambda i: (i, 0)),
            _resident((1, D_MODEL)),
            _layer_block((D_MODEL, 2 * D_FF), layer),
            _layer_block((D_FF, D_MODEL), layer),
        ],
        out_specs=pl.BlockSpec((FFN_TM, D_MODEL), lambda i: (i, 0)),
        out_shape=jax.ShapeDtypeStruct((n, D_MODEL), F32),
        scratch_shapes=[pltpu.VMEM((FFN_TM, D_FF), BF16)],
        compiler_params=pltpu.CompilerParams(
            dimension_semantics=("parallel",), vmem_limit_bytes=VMEM_LIMIT_BYTES_V7X),
        name="ffn",
    )(x2d, gain.reshape(1, D_MODEL), w_in, w_out)


def _rope(y, cos, sin_signed):
    return y * cos + pltpu.roll(y, HEAD_DIM // 2, axis=1) * sin_signed


N_GATE_BLOCKS = 2 * D_MODEL // PROJ_NC
N_ATT_BLOCKS = 3 * ATT_GROUPS


def _proj_kernel(x_ref, g_ref, qn_ref, kn_ref, wh_ref, *refs):
    wg_refs, refs = refs[:N_GATE_BLOCKS], refs[N_GATE_BLOCKS:]
    wa_refs, refs = refs[:N_ATT_BLOCKS], refs[N_ATT_BLOCKS:]
    tab_refs, hg_outs, att_outs, slab_ref = refs[:6], refs[6:12], refs[12:21], refs[21]
    h = (_rms_rows(x_ref[...]) * g_ref[...]).astype(BF16)
    scale = HEAD_DIM ** -0.5 * math.log2(math.e)

    def att_section(g, s):
        d = ATT_DILATIONS[g]
        cos_ref, sin_ref = tab_refs[2 * g:2 * g + 2]
        o_ref = att_outs[3 * g + s]
        rows = PROJ_TM // d
        y = jnp.dot(h, wa_refs[s * ATT_GROUPS + g][...], preferred_element_type=F32)
        for hd in range(ATT_HEADS):
            cs = slice(hd * HEAD_DIM, (hd + 1) * HEAD_DIM)
            if d > 1:
                slab = slab_ref.at[(g - 1) * 3 + s, hd]
                slab[...] = y[:, cs]
                yh = jnp.concatenate([slab[pl.ds(r, rows, stride=d), :] for r in range(d)], axis=0)
            else:
                yh = y[:, cs]
            if s < 2:
                gain = (qn_ref if s == 0 else kn_ref)[g:g + 1, :]
                yh = _rope(_rms_rows(yh) * gain, cos_ref[...].reshape(PROJ_TM, HEAD_DIM),
                           sin_ref[...].reshape(PROJ_TM, HEAD_DIM))
                if s == 0:
                    yh = yh * scale
            o_ref[0, :, :, cs] = yh.astype(BF16).reshape(d, rows, HEAD_DIM)

    hg_fns = (_silu, None, None, _silu, jax.nn.sigmoid, jax.nn.sigmoid)

    def hg_section(s, c):
        lo = c * PROJ_NC
        if s < 4:
            w = wh_ref[:, s * HG_W + lo:s * HG_W + lo + PROJ_NC]
        else:
            w = wg_refs[(s - 4) * (D_MODEL // PROJ_NC) + c][...]
        y = jnp.dot(h, w, preferred_element_type=F32)
        if hg_fns[s] is not None:
            y = hg_fns[s](y)
        hg_outs[s][:, lo:lo + PROJ_NC] = y.astype(hg_outs[s].dtype)

    att = [(g, s) for g in range(ATT_GROUPS) for s in range(3)]
    hgs = [(s, c) for s in range(len(hg_fns)) for c in range(HG_W // PROJ_NC)]
    while att or hgs:
        if att:
            att_section(*att.pop(0))
        for _ in range(2 if len(hgs) > len(att) else 1):
            if hgs:
                hg_section(*hgs.pop(0))


def _proj(x3d, gain, w_in, layer, qn, kn, tables):
    b, t, _ = x3d.shape
    nt = t // PROJ_TM
    in_specs = [
        pl.BlockSpec((None, PROJ_TM, D_MODEL), lambda bi, i: (bi, i, 0)),
        _resident((1, D_MODEL)),
        _resident((ATT_GROUPS, HEAD_DIM)),
        _resident((ATT_GROUPS, HEAD_DIM)),
        _layer_block((D_MODEL, 4 * HG_W), layer),
    ]
    in_specs += [_layer_block((D_MODEL, PROJ_NC), layer, COL_GA // PROJ_NC + c) for c in range(N_GATE_BLOCKS)]
    in_specs += [_layer_block((D_MODEL, ATT_GW), layer, COL_AQ // ATT_GW + k) for k in range(N_ATT_BLOCKS)]
    for d in ATT_DILATIONS:
        for _ in range(2):
            in_specs.append(pl.BlockSpec((d, PROJ_TM // d, HEAD_DIM), lambda bi, i: (0, i, 0)))
    tile = lambda: pl.BlockSpec((PROJ_TM, HG_W), lambda bi, i: (bi * nt + i, 0))
    bf = jax.ShapeDtypeStruct((b * t, HG_W), BF16)
    out_specs = [tile() for _ in range(6)]
    out_shape = [bf, jax.ShapeDtypeStruct((b * t, HG_W), F32), bf, bf, bf, bf]
    for d in ATT_DILATIONS:
        for _ in range(3):
            out_specs.append(pl.BlockSpec((1, d, PROJ_TM // d, ATT_GW), lambda bi, i: (bi, 0, i, 0)))
            out_shape.append(jax.ShapeDtypeStruct((b, d, t // d, ATT_GW), BF16))
    outs = pl.pallas_call(
        _proj_kernel,
        grid=(b, nt),
        in_specs=in_specs,
        out_specs=out_specs,
        out_shape=out_shape,
        scratch_shapes=[pltpu.VMEM((2 * 3, ATT_HEADS, PROJ_TM, HEAD_DIM), F32)],
        compiler_params=pltpu.CompilerParams(
            dimension_semantics=("parallel", "parallel"), vmem_limit_bytes=VMEM_LIMIT_BYTES_V7X),
        name="proj",
    )(x3d, gain.reshape(1, D_MODEL), qn, kn, w_in, *([w_in] * (N_GATE_BLOCKS + N_ATT_BLOCKS)), *tables)
    return outs[:6], outs[6:]


def _rope_tables(t):
    pos = np.arange(t, dtype=np.float64)
    inv = ROPE_THETA ** (-np.arange(0, HEAD_DIM, 2, dtype=np.float64) / HEAD_DIM)
    ang = pos[:, None] * inv[None, :]
    cos = np.concatenate([np.cos(ang), np.cos(ang)], axis=-1).astype(np.float32)
    sin = np.concatenate([-np.sin(ang), np.sin(ang)], axis=-1).astype(np.float32)
    tables = []
    for d in ATT_DILATIONS:
        for tab in (cos, sin):
            tables.append(jnp.asarray(np.ascontiguousarray(tab.reshape(t // d, d, HEAD_DIM).transpose(1, 0, 2))))
    return tables


def _hgrn_levels(c):
    levels = []
    m = 1
    while m < c:
        levels.append(m)
        m *= 2
    return levels


def _hgrn_kernel(layer, qa_ref, hf_ref, va_ref, gate_ref, lbl_ref, on_ref, o_ref, state_ref):
    c = HG_C

    @pl.when(pl.program_id(1) == 0)
    def _():
        state_ref[...] = jnp.zeros_like(state_ref)

    logits = lbl_ref[...]
    e = jnp.exp(logits - jnp.max(logits, axis=0, keepdims=True))
    sm = e / jnp.sum(e, axis=0, keepdims=True)
    lb_all = jnp.sum(sm[1:layer + 1], axis=0, keepdims=True) if layer > 0 else jnp.zeros((1, HG_W), F32)

    row = lax.broadcasted_iota(jnp.int32, (c, c), 0)
    colm = lax.broadcasted_iota(jnp.int32, (c, c), 1)
    tri = (colm <= row).astype(BF16)
    rix = lax.broadcasted_iota(jnp.int32, (c, HG_W), 0)
    sub = lax.broadcasted_iota(jnp.int32, (c // SUBLANES, SUBLANES, HG_W), 1)
    levels = _hgrn_levels(c)
    nrb = c // SUBLANES
    rblk = lambda a, i: a[i * SUBLANES:(i + 1) * SUBLANES]
    diag = [rblk(row == colm, i) for i in range(nrb)]
    sgns = {m: jnp.where(rix % (2 * m) >= m, 1.0, -1.0).astype(F32) for m in levels if 1 < m < BF16_ROWS}
    sels = {}
    for m in levels:
        sel = ((row // (2 * m)) == (colm // (2 * m))) & ((row % (2 * m)) >= m) & ((colm % (2 * m)) < m)
        sels[m] = [rblk(sel, i) for i in range(nrb)]
    odd_row = (rix % 2) == 1
    heads = [slice(hd * HEAD_DIM, (hd + 1) * HEAD_DIM) for hd in range(HG_HEADS)]
    nt_dims = (((1,), (1,)), ((), ()))

    def gates(ci):
        rows = pl.ds(ci * c, c)
        qb = qa_ref[rows, :]
        vb = va_ref[rows, :]
        f = lb_all + (1.0 - lb_all) * jax.nn.sigmoid(hf_ref[rows, :])
        k = 1.0 - f
        kb = k.astype(BF16)
        g = jnp.log2(f)
        g1 = g.astype(BF16)
        rem = g - g1.astype(F32)
        g2 = rem.astype(BF16)
        g3 = (rem - g2.astype(F32)).astype(BF16)
        gc = (jnp.dot(tri, g1, preferred_element_type=F32)
              + jnp.dot(tri, g2, preferred_element_type=F32)
              + jnp.dot(tri, g3, preferred_element_type=F32))
        g_last = gc[c - 1:c, :]
        qi = qb * jnp.exp2(gc).astype(BF16)
        ks = kb * jnp.exp2(g_last - gc).astype(BF16)
        decay = jnp.exp2(g_last)
        q1 = qb * jnp.where(odd_row, f, 1.0).astype(BF16)
        return rows, qb, vb, kb, gc, qi, ks, decay, q1

    def mix(rows, qb, vb, kb, gc, qi, ks, decay, q1):
        outs, accs = [], []
        for hd, cs in enumerate(heads):
            st = state_ref[hd]
            outs.append(lax.dot_general(qi[:, cs], st.astype(BF16), nt_dims, preferred_element_type=F32))
            upd = lax.dot_general(vb[:, cs], ks[:, cs], (((0,), (0,)), ((), ())), preferred_element_type=F32)
            state_ref[hd] = st * decay[:, cs] + upd
            p_d = lax.dot_general(jnp.concatenate([qb[:, cs], q1[:, cs]], axis=0), kb[:, cs], nt_dims,
                                  preferred_element_type=F32)
            accs.append([jnp.where(diag[i], rblk(p_d, i), jnp.where(sels[1][i], rblk(p_d, nrb + i), 0.0))
                         for i in range(nrb)])
        for m in levels[1:]:
            if m >= BF16_ROWS:
                kparts, qparts, qblocks = [], [], []
                for p in range(c // (2 * m)):
                    lo, mid, hi = p * 2 * m, p * 2 * m + m, (p + 1) * 2 * m
                    gb = gc[mid - 1:mid, :]
                    kparts.append(kb[lo:mid] * jnp.exp2(gb - gc[lo:mid]).astype(BF16))
                    kparts.append(kb[mid:hi])
                    qparts.append(qb[mid:hi] * jnp.exp2(gc[mid:hi] - gb).astype(BF16))
                    qblocks += list(range(mid // SUBLANES, hi // SUBLANES))
                kl = jnp.concatenate(kparts[:-1], axis=0)
                ql = jnp.concatenate(qparts, axis=0)
            else:
                qblocks = list(range(nrb))
                grp = max(SUBLANES, 2 * m)
                g3d = gc.reshape(c // grp, grp, HG_W)
                bnd3 = None
                for p in range(grp // (2 * m)):
                    src = jnp.broadcast_to(g3d[:, p * 2 * m + m - 1:p * 2 * m + m, :], (c // grp, grp, HG_W))
                    bnd3 = src if bnd3 is None else jnp.where(sub // (2 * m) == p, src, bnd3)
                ex = jnp.exp2((gc - bnd3.reshape(c, HG_W)) * sgns[m]).astype(BF16)
                ql = qb * ex
                kl = kb * ex
            for hd, cs in enumerate(heads):
                p_l = lax.dot_general(ql[:, cs], kl[:, cs], nt_dims, preferred_element_type=F32)
                n_keys = kl.shape[0]
                for n, i in enumerate(qblocks):
                    blk = rblk(p_l, n)
                    if n_keys < c:
                        blk = jnp.concatenate([blk, jnp.zeros((SUBLANES, c - n_keys), F32)], axis=1)
                    accs[hd][i] = jnp.where(sels[m][i], blk, accs[hd][i])
        return rows, vb, outs, accs

    def emit(rows, vb, outs, accs):
        for hd, cs in enumerate(heads):
            a = jnp.concatenate(accs[hd], axis=0).astype(BF16)
            o = outs[hd] + jnp.dot(a, vb[:, cs], preferred_element_type=F32)
            o = _rms_rows(o) * on_ref[:, cs] * gate_ref[rows, cs].astype(F32)
            o_ref[rows, cs] = o.astype(BF16)

    n_chunks = HG_TB // c
    staged, pending = gates(0), None
    for ci in range(n_chunks):
        mixed = mix(*staged)
        staged = gates(ci + 1) if ci + 1 < n_chunks else None
        if pending is not None:
            emit(*pending)
        pending = mixed
    emit(*pending)


def _hgrn(layer, qa, hf, va, gate, lb_logits, out_norm, b, t):
    nt = t // HG_TB
    tile = lambda: pl.BlockSpec((HG_TB, HG_W), lambda bi, i: (bi * nt + i, 0))
    return pl.pallas_call(
        functools.partial(_hgrn_kernel, layer),
        grid=(b, nt),
        in_specs=[tile(), tile(), tile(), tile(), _resident((DEPTH, HG_W)), _resident((1, HG_W))],
        out_specs=tile(),
        out_shape=jax.ShapeDtypeStruct((b * t, HG_W), BF16),
        scratch_shapes=[pltpu.VMEM((HG_HEADS, HEAD_DIM, HEAD_DIM), F32)],
        compiler_params=pltpu.CompilerParams(
            dimension_semantics=("parallel", "arbitrary"), vmem_limit_bytes=VMEM_LIMIT_BYTES_V7X),
        name="hgrn",
    )(qa, hf, va, gate, lb_logits, out_norm.reshape(1, HG_W))


def _attn_kernel(*refs):
    ins, o_ref, osc_ref, lsc_ref = refs[:15], refs[15], refs[16], refs[17]
    qi = lax.broadcasted_iota(jnp.int32, (ATT_BACK, 2 * ATT_BACK), 0)
    ki = lax.broadcasted_iota(jnp.int32, (ATT_BACK, 2 * ATT_BACK), 1)
    band = (ki >= qi) & (ki <= qi + ATT_BACK)
    first_key = jnp.where(pl.program_id(2) == 0, ATT_BACK, 0)
    band_first = band & (ki >= first_key)
    for g, d in enumerate(ATT_DILATIONS):
        q_ref, kc_ref, kp_ref, vc_ref, vp_ref = ins[5 * g:5 * g + 5]
        nblk = ATT_TT // d // ATT_BACK
        for r in range(d):
            for j in range(nblk):
                q = q_ref[r, j * ATT_BACK:(j + 1) * ATT_BACK, :]
                if j == 0:
                    kk = jnp.concatenate([kp_ref[r], kc_ref[r, 0:ATT_BACK, :]], axis=0)
                    vv = jnp.concatenate([vp_ref[r], vc_ref[r, 0:ATT_BACK, :]], axis=0)
                    mask = band_first
                else:
                    kk = kc_ref[r, (j - 1) * ATT_BACK:(j + 1) * ATT_BACK, :]
                    vv = vc_ref[r, (j - 1) * ATT_BACK:(j + 1) * ATT_BACK, :]
                    mask = band
                s = lax.dot_general(q, kk, (((1,), (1,)), ((), ())), preferred_element_type=F32)
                s = jnp.where(mask, s, -jnp.inf)
                mx = jnp.max(s, axis=-1, keepdims=True)
                p = jnp.exp2(s - mx)
                den = jnp.sum(p, axis=-1, keepdims=True)
                o = jnp.dot(p.astype(BF16), vv, preferred_element_type=F32) / den
                lse = mx + jnp.log2(den)
                dst = pl.ds(j * ATT_BACK * d + r, ATT_BACK, stride=d) if d > 1 else pl.ds(j * ATT_BACK, ATT_BACK)
                osc_ref[g, dst, :] = o
                lsc_ref[g, dst, :] = jnp.broadcast_to(lse, (ATT_BACK, HEAD_DIM))
    for j in range(ATT_TT // ATT_BACK):
        rs = slice(j * ATT_BACK, (j + 1) * ATT_BACK)
        l0, l1, l2 = lsc_ref[0, rs, :], lsc_ref[1, rs, :], lsc_ref[2, rs, :]
        mx = jnp.maximum(jnp.maximum(l0, l1), l2)
        w0, w1, w2 = jnp.exp2(l0 - mx), jnp.exp2(l1 - mx), jnp.exp2(l2 - mx)
        o = (w0 * osc_ref[0, rs, :] + w1 * osc_ref[1, rs, :] + w2 * osc_ref[2, rs, :]) / (w0 + w1 + w2)
        o_ref[rs, :] = o.astype(BF16)


def _attn(qkv, b, t):
    nt = t // ATT_TT
    in_specs, args = [], []
    for g, d in enumerate(ATT_DILATIONS):
        rows = ATT_TT // d
        nb = rows // ATT_BACK
        cur = lambda rows=rows, d=d: pl.BlockSpec((None, d, rows, HEAD_DIM), lambda bi, h, i: (bi, 0, i, h))
        prev = lambda nb=nb, d=d: pl.BlockSpec(
            (None, d, ATT_BACK, HEAD_DIM), lambda bi, h, i: (bi, 0, jnp.maximum(i * nb - 1, 0), h))
        q, k, v = qkv[g]
        in_specs += [cur(), cur(), prev(), cur(), prev()]
        args += [q, k, k, v, v]
    return pl.pallas_call(
        _attn_kernel,
        grid=(b, ATT_HEADS, nt),
        in_specs=in_specs,
        out_specs=pl.BlockSpec((None, ATT_TT, HEAD_DIM), lambda bi, h, i: (bi, i, h)),
        out_shape=jax.ShapeDtypeStruct((b, t, ATT_GW), BF16),
        scratch_shapes=[pltpu.VMEM((ATT_GROUPS, ATT_TT, HEAD_DIM), F32),
                        pltpu.VMEM((ATT_GROUPS, ATT_TT, HEAD_DIM), F32)],
        compiler_params=pltpu.CompilerParams(
            dimension_semantics=("parallel", "parallel", "arbitrary"), vmem_limit_bytes=VMEM_LIMIT_BYTES_V7X),
        name="attn",
    )(*args)


def _merge_kernel(x_ref, oa_ref, ob_ref, sga_ref, sgb_ref, wa_ref, wb_ref, wo_ref, o_ref):
    ya = jnp.dot(oa_ref[...], wa_ref[...], preferred_element_type=F32)
    yb = jnp.dot(ob_ref[...], wb_ref[...], preferred_element_type=F32)
    merged = sga_ref[...].astype(F32) * ya + sgb_ref[...].astype(F32) * yb
    o_ref[...] = x_ref[...] + jnp.dot(merged.astype(BF16), wo_ref[...], preferred_element_type=F32)


def _merge(x2d, oa, ob, sga, sgb, wa, wb, wo, layer):
    n = x2d.shape[0]
    tile = lambda w: pl.BlockSpec((MERGE_TM, w), lambda i: (i, 0))
    return pl.pallas_call(
        _merge_kernel,
        grid=(n // MERGE_TM,),
        in_specs=[tile(D_MODEL), tile(HG_W), tile(ATT_GW), tile(D_MODEL), tile(D_MODEL),
                  _layer_block((HG_W, D_MODEL), layer), _layer_block((ATT_GW, D_MODEL), layer),
                  _layer_block((D_MODEL, D_MODEL), layer)],
        out_specs=tile(D_MODEL),
        out_shape=jax.ShapeDtypeStruct((n, D_MODEL), F32),
        compiler_params=pltpu.CompilerParams(
            dimension_semantics=("parallel",), vmem_limit_bytes=VMEM_LIMIT_BYTES_V7X),
        name="merge",
    )(x2d, oa, ob, sga, sgb, wa, wb, wo)


def kernel(x, ffn1_norm, ffn1_w_in, ffn1_w_out, mix_norm, w_in, hgrn_lb_logits, hgrn_out_norm, attn_q_norm,
           attn_k_norm, w_branch_a, w_branch_b, w_out, ffn2_norm, ffn2_w_in, ffn2_w_out):
    b, t, _ = x.shape
    assert x.shape[2] == D_MODEL and t % ATT_TT == 0 and t % HG_TB == 0
    assert COL_GA % PROJ_NC == 0 and COL_AQ % ATT_GW == 0
    tables = _rope_tables(t)
    ffn1_w_in, ffn1_w_out, w_in, w_branch_a, w_branch_b, w_out, ffn2_w_in, ffn2_w_out = (
        w.astype(BF16) for w in (ffn1_w_in, ffn1_w_out, w_in, w_branch_a, w_branch_b, w_out, ffn2_w_in, ffn2_w_out))
    x2d = x.reshape(b * t, D_MODEL)
    for l in range(DEPTH):
        x2d = _ffn(x2d, ffn1_norm[l], ffn1_w_in, ffn1_w_out, l)

        (qa, hf, va, gate, sga, sgb), qkv = _proj(
            x2d.reshape(b, t, D_MODEL), mix_norm[l], w_in, l, attn_q_norm[l], attn_k_norm[l], tables)
        oa = _hgrn(l, qa, hf, va, gate, hgrn_lb_logits, hgrn_out_norm[l], b, t)
        ob = _attn([qkv[3 * g:3 * g + 3] for g in range(ATT_GROUPS)], b, t)
        x2d = _merge(x2d, oa, ob.reshape(b * t, ATT_GW), sga, sgb, w_branch_a, w_branch_b, w_out, l)

        x2d = _ffn(x2d, ffn2_norm[l], ffn2_w_in, ffn2_w_out, l)
    return x2d.reshape(b, t, D_MODEL)
```

```python
import functools
import math

import jax
import jax.numpy as jnp
import numpy as np
from jax import lax
from jax.experimental import pallas as pl
from jax.experimental.pallas import tpu as pltpu

F32 = jnp.float32
BF16 = jnp.bfloat16

D_MODEL = 1024
DEPTH = 2
D_FF = 2816
HEAD_DIM = 128
HG_HEADS = D_MODEL // HEAD_DIM
HG_W = HG_HEADS * HEAD_DIM
ATT_DILATIONS = (1, 4, 16)
ATT_BACK = 128
ATT_GROUPS = 3
ATT_HEADS = 4
ATT_GW = ATT_HEADS * HEAD_DIM
ATT_W = ATT_GROUPS * ATT_GW
ROPE_THETA = 10000.0
EPS = 1e-6

COL_HQ, COL_HF, COL_HI, COL_HG = 0, HG_W, 2 * HG_W, 3 * HG_W
COL_AQ = 4 * HG_W
COL_AK = COL_AQ + ATT_W
COL_AV = COL_AK + ATT_W
COL_GA = COL_AV + ATT_W
COL_GB = COL_GA + D_MODEL
P_IN = COL_GB + D_MODEL

VMEM_LIMIT_BYTES_V7X = 56 * 1024 * 1024
SUBLANES = 8
BF16_ROWS = 16

FFN_TM = 512
FFN_FC = 256
PROJ_TM = 256
PROJ_NC = 512
HG_TB = 1024
HG_C = 128
ATT_TT = ATT_BACK * ATT_DILATIONS[-1]
ATT_HPS = 2
MERGE_TM = 1024


def _resident(shape):
    nd = len(shape)
    return pl.BlockSpec(shape, lambda *_: (0,) * nd, pipeline_mode=pl.Buffered(1))


def _layer_block(shape, layer, col_block=0):
    return pl.BlockSpec((None,) + tuple(shape), lambda *_: (layer, 0, col_block), pipeline_mode=pl.Buffered(1))


def _rms_rows(x):
    return x * lax.rsqrt(jnp.mean(x * x, axis=-1, keepdims=True) + EPS)


def _silu(x):
    return x * jax.nn.sigmoid(x)


def _ffn_kernel(x_ref, g_ref, win_ref, wout_ref, o_ref, act_ref):
    x = x_ref[...]
    h = (_rms_rows(x) * g_ref[...]).astype(BF16)
    for c in range(D_FF // FFN_FC):
        lo = c * FFN_FC
        a = jnp.dot(h, win_ref[:, lo:lo + FFN_FC], preferred_element_type=F32)
        b = jnp.dot(h, win_ref[:, D_FF + lo:D_FF + lo + FFN_FC], preferred_element_type=F32)
        act_ref[:, lo:lo + FFN_FC] = (_silu(a) * b).astype(BF16)
    y = jnp.dot(act_ref[...], wout_ref[...], preferred_element_type=F32)
    o_ref[...] = x + 0.5 * y


def _ffn(x2d, gain, w_in, w_out, layer):
    n = x2d.shape[0]
    return pl.pallas_call(
        _ffn_kernel,
        grid=(n // FFN_TM,),
        in_specs=[
            pl.BlockSpec((FFN_TM, D_MODEL), lambda i: (i, 0)),
            _resident((1, D_MODEL)),
            _layer_block((D_MODEL, 2 * D_FF), layer),
            _layer_block((D_FF, D_MODEL), layer),
        ],
        out_specs=pl.BlockSpec((FFN_TM, D_MODEL), lambda i: (i, 0)),
        out_shape=jax.ShapeDtypeStruct((n, D_MODEL), F32),
        scratch_shapes=[pltpu.VMEM((FFN_TM, D_FF), BF16)],
        compiler_params=pltpu.CompilerParams(
            dimension_semantics=("parallel",), vmem_limit_bytes=VMEM_LIMIT_BYTES_V7X),
        name="ffn",
    )(x2d, gain.reshape(1, D_MODEL), w_in, w_out)


def _rope(y, cos, sin_signed):
    return y * cos + pltpu.roll(y, HEAD_DIM // 2, axis=1) * sin_signed


N_GATE_BLOCKS = 2 * D_MODEL // PROJ_NC
N_ATT_BLOCKS = 3 * ATT_GROUPS


def _proj_kernel(x_ref, g_ref, qn_ref, kn_ref, wh_ref, *refs):
    wg_refs, refs = refs[:N_GATE_BLOCKS], refs[N_GATE_BLOCKS:]
    wa_refs, refs = refs[:N_ATT_BLOCKS], refs[N_ATT_BLOCKS:]
    tab_refs, hg_outs, att_outs, slab_ref = refs[:6], refs[6:12], refs[12:21], refs[21]
    h = (_rms_rows(x_ref[...]) * g_ref[...]).astype(BF16)
    scale = HEAD_DIM ** -0.5 * math.log2(math.e)

    def att_section(g, s):
        d = ATT_DILATIONS[g]
        cos_ref, sin_ref = tab_refs[2 * g:2 * g + 2]
        o_ref = att_outs[3 * g + s]
        rows = PROJ_TM // d
        y = jnp.dot(h, wa_refs[s * ATT_GROUPS + g][...], preferred_element_type=F32)
        for hd in range(ATT_HEADS):
            cs = slice(hd * HEAD_DIM, (hd + 1) * HEAD_DIM)
            if d > 1:
                slab = slab_ref.at[(g - 1) * 3 + s, hd]
                slab[...] = y[:, cs]
                yh = jnp.concatenate([slab[pl.ds(r, rows, stride=d), :] for r in range(d)], axis=0)
            else:
                yh = y[:, cs]
            if s < 2:
                gain = (qn_ref if s == 0 else kn_ref)[g:g + 1, :]
                yh = _rope(_rms_rows(yh) * gain, cos_ref[...].reshape(PROJ_TM, HEAD_DIM),
                           sin_ref[...].reshape(PROJ_TM, HEAD_DIM))
                if s == 0:
                    yh = yh * scale
            o_ref[0, :, :, cs] = yh.astype(BF16).reshape(d, rows, HEAD_DIM)

    hg_fns = (_silu, None, None, _silu, jax.nn.sigmoid, jax.nn.sigmoid)

    def hg_section(s, c):
        lo = c * PROJ_NC
        if s < 4:
            w = wh_ref[:, s * HG_W + lo:s * HG_W + lo + PROJ_NC]
        else:
            w = wg_refs[(s - 4) * (D_MODEL // PROJ_NC) + c][...]
        y = jnp.dot(h, w, preferred_element_type=F32)
        if hg_fns[s] is not None:
            y = hg_fns[s](y)
        hg_outs[s][:, lo:lo + PROJ_NC] = y.astype(hg_outs[s].dtype)

    att = [(g, s) for g in reversed(range(ATT_GROUPS)) for s in range(3)]
    hgs = [(s, c) for s in range(len(hg_fns)) for c in range(HG_W // PROJ_NC)]
    while att or hgs:
        if att:
            att_section(*att.pop(0))
        for _ in range(2 if len(hgs) > len(att) else 1):
            if hgs:
                hg_section(*hgs.pop(0))


def _proj(x3d, gain, w_in, layer, qn, kn, tables):
    b, t, _ = x3d.shape
    nt = t // PROJ_TM
    in_specs = [
        pl.BlockSpec((None, PROJ_TM, D_MODEL), lambda bi, i: (bi, i, 0)),
        _resident((1, D_MODEL)),
        _resident((ATT_GROUPS, HEAD_DIM)),
        _resident((ATT_GROUPS, HEAD_DIM)),
        _layer_block((D_MODEL, 4 * HG_W), layer),
    ]
    in_specs += [_layer_block((D_MODEL, PROJ_NC), layer, COL_GA // PROJ_NC + c) for c in range(N_GATE_BLOCKS)]
    in_specs += [_layer_block((D_MODEL, ATT_GW), layer, COL_AQ // ATT_GW + k) for k in range(N_ATT_BLOCKS)]
    for d in ATT_DILATIONS:
        for _ in range(2):
            in_specs.append(pl.BlockSpec((d, PROJ_TM // d, HEAD_DIM), lambda bi, i: (0, i, 0)))
    tile = lambda: pl.BlockSpec((PROJ_TM, HG_W), lambda bi, i: (bi * nt + i, 0))
    bf = jax.ShapeDtypeStruct((b * t, HG_W), BF16)
    out_specs = [tile() for _ in range(6)]
    out_shape = [bf, jax.ShapeDtypeStruct((b * t, HG_W), F32), bf, bf, bf, bf]
    for d in ATT_DILATIONS:
        for _ in range(3):
            out_specs.append(pl.BlockSpec((1, d, PROJ_TM // d, ATT_GW), lambda bi, i: (bi, 0, i, 0)))
            out_shape.append(jax.ShapeDtypeStruct((b, d, t // d, ATT_GW), BF16))
    outs = pl.pallas_call(
        _proj_kernel,
        grid=(b, nt),
        in_specs=in_specs,
        out_specs=out_specs,
        out_shape=out_shape,
        scratch_shapes=[pltpu.VMEM((2 * 3, ATT_HEADS, PROJ_TM, HEAD_DIM), F32)],
        compiler_params=pltpu.CompilerParams(
            dimension_semantics=("parallel", "parallel"), vmem_limit_bytes=VMEM_LIMIT_BYTES_V7X),
        name="proj",
    )(x3d, gain.reshape(1, D_MODEL), qn, kn, w_in, *([w_in] * (N_GATE_BLOCKS + N_ATT_BLOCKS)), *tables)
    return outs[:6], outs[6:]


def _rope_tables(t):
    pos = np.arange(t, dtype=np.float64)
    inv = ROPE_THETA ** (-np.arange(0, HEAD_DIM, 2, dtype=np.float64) / HEAD_DIM)
    ang = pos[:, None] * inv[None, :]
    cos = np.concatenate([np.cos(ang), np.cos(ang)], axis=-1).astype(np.float32)
    sin = np.concatenate([-np.sin(ang), np.sin(ang)], axis=-1).astype(np.float32)
    tables = []
    for d in ATT_DILATIONS:
        for tab in (cos, sin):
            tables.append(jnp.asarray(np.ascontiguousarray(tab.reshape(t // d, d, HEAD_DIM).transpose(1, 0, 2))))
    return tables


def _hgrn_levels(c):
    levels = []
    m = 1
    while m < c:
        levels.append(m)
        m *= 2
    return levels


def _hgrn_kernel(layer, qa_ref, hf_ref, va_ref, gate_ref, lbl_ref, on_ref, o_ref, state_ref):
    c = HG_C

    @pl.when(pl.program_id(1) == 0)
    def _():
        state_ref[...] = jnp.zeros_like(state_ref)

    logits = lbl_ref[...]
    e = jnp.exp(logits - jnp.max(logits, axis=0, keepdims=True))
    sm = e / jnp.sum(e, axis=0, keepdims=True)
    lb_all = jnp.sum(sm[1:layer + 1], axis=0, keepdims=True) if layer > 0 else jnp.zeros((1, HG_W), F32)

    row = lax.broadcasted_iota(jnp.int32, (c, c), 0)
    colm = lax.broadcasted_iota(jnp.int32, (c, c), 1)
    tri = (colm <= row).astype(BF16)
    rix = lax.broadcasted_iota(jnp.int32, (c, HG_W), 0)
    sub = lax.broadcasted_iota(jnp.int32, (c // SUBLANES, SUBLANES, HG_W), 1)
    levels = _hgrn_levels(c)
    nrb = c // SUBLANES
    rblk = lambda a, i: a[i * SUBLANES:(i + 1) * SUBLANES]
    diag = [rblk(row == colm, i) for i in range(nrb)]
    sgns = {m: jnp.where(rix % (2 * m) >= m, 1.0, -1.0).astype(F32) for m in levels if 1 < m < BF16_ROWS}
    sels = {}
    for m in levels:
        sel = ((row // (2 * m)) == (colm // (2 * m))) & ((row % (2 * m)) >= m) & ((colm % (2 * m)) < m)
        sels[m] = [rblk(sel, i) for i in range(nrb)]
    odd_row = (rix % 2) == 1
    heads = [slice(hd * HEAD_DIM, (hd + 1) * HEAD_DIM) for hd in range(HG_HEADS)]
    nt_dims = (((1,), (1,)), ((), ()))

    def gates(ci):
        rows = pl.ds(ci * c, c)
        qb = qa_ref[rows, :]
        vb = va_ref[rows, :]
        f = lb_all + (1.0 - lb_all) * jax.nn.sigmoid(hf_ref[rows, :])
        k = 1.0 - f
        kb = k.astype(BF16)
        g = jnp.log2(f)
        g1 = g.astype(BF16)
        rem = g - g1.astype(F32)
        g2 = rem.astype(BF16)
        g3 = (rem - g2.astype(F32)).astype(BF16)
        gc = (jnp.dot(tri, g1, preferred_element_type=F32)
              + jnp.dot(tri, g2, preferred_element_type=F32)
              + jnp.dot(tri, g3, preferred_element_type=F32))
        g_last = gc[c - 1:c, :]
        qi = qb * jnp.exp2(gc).astype(BF16)
        ks = kb * jnp.exp2(g_last - gc).astype(BF16)
        decay = jnp.exp2(g_last)
        q1 = qb * jnp.where(odd_row, f, 1.0).astype(BF16)
        return rows, qb, vb, kb, gc, qi, ks, decay, q1

    def mix(rows, qb, vb, kb, gc, qi, ks, decay, q1):
        outs, accs = [], []
        for hd, cs in enumerate(heads):
            st = state_ref[hd]
            outs.append(lax.dot_general(qi[:, cs], st.astype(BF16), nt_dims, preferred_element_type=F32))
            upd = lax.dot_general(vb[:, cs], ks[:, cs], (((0,), (0,)), ((), ())), preferred_element_type=F32)
            state_ref[hd] = st * decay[:, cs] + upd
            p_d = lax.dot_general(jnp.concatenate([qb[:, cs], q1[:, cs]], axis=0), kb[:, cs], nt_dims,
                                  preferred_element_type=F32)
            accs.append([jnp.where(diag[i], rblk(p_d, i), jnp.where(sels[1][i], rblk(p_d, nrb + i), 0.0))
                         for i in range(nrb)])
        for m in levels[1:]:
            if m >= BF16_ROWS:
                kparts, qparts, qblocks = [], [], []
                for p in range(c // (2 * m)):
                    lo, mid, hi = p * 2 * m, p * 2 * m + m, (p + 1) * 2 * m
                    gb = gc[mid - 1:mid, :]
                    kparts.append(kb[lo:mid] * jnp.exp2(gb - gc[lo:mid]).astype(BF16))
                    kparts.append(kb[mid:hi])
                    qparts.append(qb[mid:hi] * jnp.exp2(gc[mid:hi] - gb).astype(BF16))
                    qblocks += list(range(mid // SUBLANES, hi // SUBLANES))
                kl = jnp.concatenate(kparts[:-1], axis=0)
                ql = jnp.concatenate(qparts, axis=0)
            else:
                qblocks = list(range(nrb))
                grp = max(SUBLANES, 2 * m)
                g3d = gc.reshape(c // grp, grp, HG_W)
                bnd3 = None
                for p in range(grp // (2 * m)):
                    src = jnp.broadcast_to(g3d[:, p * 2 * m + m - 1:p * 2 * m + m, :], (c // grp, grp, HG_W))
                    bnd3 = src if bnd3 is None else jnp.where(sub // (2 * m) == p, src, bnd3)
                ex = jnp.exp2((gc - bnd3.reshape(c, HG_W)) * sgns[m]).astype(BF16)
                ql = qb * ex
                kl = kb * ex
            for hd, cs in enumerate(heads):
                p_l = lax.dot_general(ql[:, cs], kl[:, cs], nt_dims, preferred_element_type=F32)
                n_keys = kl.shape[0]
                for n, i in enumerate(qblocks):
                    blk = rblk(p_l, n)
                    if n_keys < c:
                        blk = jnp.concatenate([blk, jnp.zeros((SUBLANES, c - n_keys), F32)], axis=1)
                    accs[hd][i] = jnp.where(sels[m][i], blk, accs[hd][i])
        return rows, vb, outs, accs

    def emit(rows, vb, outs, accs):
        for hd, cs in enumerate(heads):
            a = jnp.concatenate(accs[hd], axis=0).astype(BF16)
            o = outs[hd] + jnp.dot(a, vb[:, cs], preferred_element_type=F32)
            o = _rms_rows(o) * on_ref[:, cs] * gate_ref[rows, cs].astype(F32)
            o_ref[rows, cs] = o.astype(BF16)

    n_chunks = HG_TB // c
    staged, pending = gates(0), None
    for ci in range(n_chunks):
        upcoming = gates(ci + 1) if ci + 1 < n_chunks else None
        mixed = mix(*staged)
        staged = upcoming
        if pending is not None:
            emit(*pending)
        pending = mixed
    emit(*pending)


def _hgrn(layer, qa, hf, va, gate, lb_logits, out_norm, b, t):
    nt = t // HG_TB
    tile = lambda: pl.BlockSpec((HG_TB, HG_W), lambda bi, i: (bi * nt + i, 0))
    return pl.pallas_call(
        functools.partial(_hgrn_kernel, layer),
        grid=(b, nt),
        in_specs=[tile(), tile(), tile(), tile(), _resident((DEPTH, HG_W)), _resident((1, HG_W))],
        out_specs=tile(),
        out_shape=jax.ShapeDtypeStruct((b * t, HG_W), BF16),
        scratch_shapes=[pltpu.VMEM((HG_HEADS, HEAD_DIM, HEAD_DIM), F32)],
        compiler_params=pltpu.CompilerParams(
            dimension_semantics=("parallel", "arbitrary"), vmem_limit_bytes=VMEM_LIMIT_BYTES_V7X),
        name="hgrn",
    )(qa, hf, va, gate, lb_logits, out_norm.reshape(1, HG_W))


def _attn_kernel(*refs):
    ins, o_ref, osc_ref, lsc_ref = refs[:15], refs[15], refs[16], refs[17]
    qi = lax.broadcasted_iota(jnp.int32, (ATT_BACK, 2 * ATT_BACK), 0)
    ki = lax.broadcasted_iota(jnp.int32, (ATT_BACK, 2 * ATT_BACK), 1)
    band = (ki >= qi) & (ki <= qi + ATT_BACK)
    first_key = jnp.where(pl.program_id(2) == 0, ATT_BACK, 0)
    band_first = band & (ki >= first_key)

    def head_blocks(hh):
        hs = slice(hh * HEAD_DIM, (hh + 1) * HEAD_DIM)
        for g, d in enumerate(ATT_DILATIONS):
            q_ref, kc_ref, kp_ref, vc_ref, vp_ref = ins[5 * g:5 * g + 5]
            nblk = ATT_TT // d // ATT_BACK
            for r in range(d):
                for j in range(nblk):
                    q = q_ref[r, j * ATT_BACK:(j + 1) * ATT_BACK, hs]
                    if j == 0:
                        kk = jnp.concatenate([kp_ref[r, :, hs], kc_ref[r, 0:ATT_BACK, hs]], axis=0)
                        vv = jnp.concatenate([vp_ref[r, :, hs], vc_ref[r, 0:ATT_BACK, hs]], axis=0)
                        mask = band_first
                    else:
                        kk = kc_ref[r, (j - 1) * ATT_BACK:(j + 1) * ATT_BACK, hs]
                        vv = vc_ref[r, (j - 1) * ATT_BACK:(j + 1) * ATT_BACK, hs]
                        mask = band
                    s = lax.dot_general(q, kk, (((1,), (1,)), ((), ())), preferred_element_type=F32)
                    s = jnp.where(mask, s, -jnp.inf)
                    mx = jnp.max(s, axis=-1, keepdims=True)
                    p = jnp.exp2(s - mx)
                    den = jnp.sum(p, axis=-1, keepdims=True)
                    o = jnp.dot(p.astype(BF16), vv, preferred_element_type=F32) / den
                    lse = mx + jnp.log2(den)
                    dst = (pl.ds(j * ATT_BACK * d + r, ATT_BACK, stride=d) if d > 1
                           else pl.ds(j * ATT_BACK, ATT_BACK))
                    osc_ref[hh, g, dst, :] = o
                    lsc_ref[hh, g, dst, :] = jnp.broadcast_to(lse, (ATT_BACK, HEAD_DIM))

    def head_merge(hh):
        for j in range(ATT_TT // ATT_BACK):
            rs = slice(j * ATT_BACK, (j + 1) * ATT_BACK)
            l0, l1, l2 = lsc_ref[hh, 0, rs, :], lsc_ref[hh, 1, rs, :], lsc_ref[hh, 2, rs, :]
            mx = jnp.maximum(jnp.maximum(l0, l1), l2)
            w0, w1, w2 = jnp.exp2(l0 - mx), jnp.exp2(l1 - mx), jnp.exp2(l2 - mx)
            o = (w0 * osc_ref[hh, 0, rs, :] + w1 * osc_ref[hh, 1, rs, :] + w2 * osc_ref[hh, 2, rs, :]) / (w0 + w1 + w2)
            o_ref[rs, hh * HEAD_DIM:(hh + 1) * HEAD_DIM] = o.astype(BF16)

    for hh in range(ATT_HPS):
        head_blocks(hh)
        if hh > 0:
            head_merge(hh - 1)
    head_merge(ATT_HPS - 1)


def _attn(qkv, b, t):
    nt = t // ATT_TT
    in_specs, args = [], []
    for g, d in enumerate(ATT_DILATIONS):
        rows = ATT_TT // d
        nb = rows // ATT_BACK
        cur = lambda rows=rows, d=d: pl.BlockSpec((None, d, rows, ATT_HPS * HEAD_DIM), lambda bi, h, i: (bi, 0, i, h))
        prev = lambda nb=nb, d=d: pl.BlockSpec(
            (None, d, ATT_BACK, ATT_HPS * HEAD_DIM), lambda bi, h, i: (bi, 0, jnp.maximum(i * nb - 1, 0), h))
        q, k, v = qkv[g]
        in_specs += [cur(), cur(), prev(), cur(), prev()]
        args += [q, k, k, v, v]
    return pl.pallas_call(
        _attn_kernel,
        grid=(b, ATT_HEADS // ATT_HPS, nt),
        in_specs=in_specs,
        out_specs=pl.BlockSpec((None, ATT_TT, ATT_HPS * HEAD_DIM), lambda bi, h, i: (bi, i, h)),
        out_shape=jax.ShapeDtypeStruct((b, t, ATT_GW), BF16),
        scratch_shapes=[pltpu.VMEM((ATT_HPS, ATT_GROUPS, ATT_TT, HEAD_DIM), F32),
                        pltpu.VMEM((ATT_HPS, ATT_GROUPS, ATT_TT, HEAD_DIM), F32)],
        compiler_params=pltpu.CompilerParams(
            dimension_semantics=("parallel", "parallel", "arbitrary"), vmem_limit_bytes=VMEM_LIMIT_BYTES_V7X),
        name="attn",
    )(*args)


def _merge_kernel(x_ref, oa_ref, ob_ref, sga_ref, sgb_ref, wa_ref, wb_ref, wo_ref, o_ref):
    ya = jnp.dot(oa_ref[...], wa_ref[...], preferred_element_type=F32)
    yb = jnp.dot(ob_ref[...], wb_ref[...], preferred_element_type=F32)
    merged = sga_ref[...].astype(F32) * ya + sgb_ref[...].astype(F32) * yb
    o_ref[...] = x_ref[...] + jnp.dot(merged.astype(BF16), wo_ref[...], preferred_element_type=F32)


def _merge(x2d, oa, ob, sga, sgb, wa, wb, wo, layer):
    n = x2d.shape[0]
    tile = lambda w: pl.BlockSpec((MERGE_TM, w), lambda i: (i, 0))
    return pl.pallas_call(
        _merge_kernel,
        grid=(n // MERGE_TM,),
        in_specs=[tile(D_MODEL), tile(HG_W), tile(ATT_GW), tile(D_MODEL), tile(D_MODEL),
                  _layer_block((HG_W, D_MODEL), layer), _layer_block((ATT_GW, D_MODEL), layer),
                  _layer_block((D_MODEL, D_MODEL), layer)],
        out_specs=tile(D_MODEL),
        out_shape=jax.ShapeDtypeStruct((n, D_MODEL), F32),
        compiler_params=pltpu.CompilerParams(
            dimension_semantics=("parallel",), vmem_limit_bytes=VMEM_LIMIT_BYTES_V7X),
        name="merge",
    )(x2d, oa, ob, sga, sgb, wa, wb, wo)


def kernel(x, ffn1_norm, ffn1_w_in, ffn1_w_out, mix_norm, w_in, hgrn_lb_logits, hgrn_out_norm, attn_q_norm,
           attn_k_norm, w_branch_a, w_branch_b, w_out, ffn2_norm, ffn2_w_in, ffn2_w_out):
    b, t, _ = x.shape
    assert x.shape[2] == D_MODEL and t % ATT_TT == 0 and t % HG_TB == 0
    assert COL_GA % PROJ_NC == 0 and COL_AQ % ATT_GW == 0
    tables = _rope_tables(t)
    ffn1_w_in, ffn1_w_out, w_in, w_branch_a, w_branch_b, w_out, ffn2_w_in, ffn2_w_out = (
        w.astype(BF16) for w in (ffn1_w_in, ffn1_w_out, w_in, w_branch_a, w_branch_b, w_out, ffn2_w_in, ffn2_w_out))
    x2d = x.reshape(b * t, D_MODEL)
    for l in range(DEPTH):
        x2d = _ffn(x2d, ffn1_norm[l], ffn1_w_in, ffn1_w_out, l)

        (qa, hf, va, gate, sga, sgb), qkv = _proj(
            x2d.reshape(b, t, D_MODEL), mix_norm[l], w_in, l, attn_q_norm[l], attn_k_norm[l], tables)
        oa = _hgrn(l, qa, hf, va, gate, hgrn_lb_logits, hgrn_out_norm[l], b, t)
        ob = _attn([qkv[3 * g:3 * g + 3] for g in range(ATT_GROUPS)], b, t)
        x2d = _merge(x2d, oa, ob.reshape(b * t, ATT_GW), sga, sgb, w_branch_a, w_branch_b, w_out, l)

        x2d = _ffn(x2d, ffn2_norm[l], ffn2_w_in, ffn2_w_out, l)
    return x2d.reshape(b, t, D_MODEL)
```

```python
import functools
import math

import jax
import jax.numpy as jnp
import numpy as np
from jax import lax
from jax.experimental import pallas as pl
from jax.experimental.pallas import tpu as pltpu

F32 = jnp.float32
BF16 = jnp.bfloat16

D_MODEL = 1024
DEPTH = 2
D_FF = 2816
HEAD_DIM = 128
HG_HEADS = D_MODEL // HEAD_DIM
HG_W = HG_HEADS * HEAD_DIM
ATT_DILATIONS = (1, 4, 16)
ATT_BACK = 128
ATT_GROUPS = 3
ATT_HEADS = 4
ATT_GW = ATT_HEADS * HEAD_DIM
ATT_W = ATT_GROUPS * ATT_GW
ROPE_THETA = 10000.0
EPS = 1e-6

COL_HQ, COL_HF, COL_HI, COL_HG = 0, HG_W, 2 * HG_W, 3 * HG_W
COL_AQ = 4 * HG_W
COL_AK = COL_AQ + ATT_W
COL_AV = COL_AK + ATT_W
COL_GA = COL_AV + ATT_W
COL_GB = COL_GA + D_MODEL
P_IN = COL_GB + D_MODEL

VMEM_LIMIT_BYTES_V7X = 56 * 1024 * 1024
SUBLANES = 8
BF16_ROWS = 16

FFN_TM = 1024
FFN_FC = 256
PROJ_TM = 256
PROJ_NC = 512
HG_TB = 1024
HG_C = 128
ATT_TT = ATT_BACK * ATT_DILATIONS[-1]
ATT_HPS = 2
MERGE_TM = 1024


def _resident(shape):
    nd = len(shape)
    return pl.BlockSpec(shape, lambda *_: (0,) * nd, pipeline_mode=pl.Buffered(1))


def _layer_block(shape, layer, col_block=0):
    return pl.BlockSpec((None,) + tuple(shape), lambda *_: (layer, 0, col_block), pipeline_mode=pl.Buffered(1))


def _rms_rows(x):
    return x * lax.rsqrt(jnp.mean(x * x, axis=-1, keepdims=True) + EPS)


def _silu(x):
    return x * jax.nn.sigmoid(x)


def _ffn_kernel(x_ref, g_ref, win_ref, wout_ref, o_ref, act_ref):
    x = x_ref[...]
    h = (_rms_rows(x) * g_ref[...]).astype(BF16)
    for c in range(D_FF // FFN_FC):
        lo = c * FFN_FC
        a = jnp.dot(h, win_ref[:, lo:lo + FFN_FC], preferred_element_type=F32)
        b = jnp.dot(h, win_ref[:, D_FF + lo:D_FF + lo + FFN_FC], preferred_element_type=F32)
        act_ref[:, lo:lo + FFN_FC] = (_silu(a) * b).astype(BF16)
    y = jnp.dot(act_ref[...], wout_ref[...], preferred_element_type=F32)
    o_ref[...] = x + 0.5 * y


def _ffn(x2d, gain, w_in, w_out, layer):
    n = x2d.shape[0]
    return pl.pallas_call(
        _ffn_kernel,
        grid=(n // FFN_TM,),
        in_specs=[
            pl.BlockSpec((FFN_TM, D_MODEL), lambda i: (i, 0)),
            _resident((1, D_MODEL)),
            _layer_block((D_MODEL, 2 * D_FF), layer),
            _layer_block((D_FF, D_MODEL), layer),
        ],
        out_specs=pl.BlockSpec((FFN_TM, D_MODEL), lambda i: (i, 0)),
        out_shape=jax.ShapeDtypeStruct((n, D_MODEL), F32),
        scratch_shapes=[pltpu.VMEM((FFN_TM, D_FF), BF16)],
        compiler_params=pltpu.CompilerParams(
            dimension_semantics=("parallel",), vmem_limit_bytes=VMEM_LIMIT_BYTES_V7X),
        name="ffn",
    )(x2d, gain.reshape(1, D_MODEL), w_in, w_out)


def _rope(y, cos, sin_signed):
    return y * cos + pltpu.roll(y, HEAD_DIM // 2, axis=1) * sin_signed


N_GATE_BLOCKS = 2 * D_MODEL // PROJ_NC
N_ATT_BLOCKS = 3 * ATT_GROUPS


def _proj_kernel(x_ref, g_ref, qn_ref, kn_ref, wh_ref, *refs):
    wg_refs, refs = refs[:N_GATE_BLOCKS], refs[N_GATE_BLOCKS:]
    wa_refs, refs = refs[:N_ATT_BLOCKS], refs[N_ATT_BLOCKS:]
    tab_refs, hg_outs, att_outs, slab_ref = refs[:6], refs[6:12], refs[12:21], refs[21]
    h = (_rms_rows(x_ref[...]) * g_ref[...]).astype(BF16)
    scale = HEAD_DIM ** -0.5 * math.log2(math.e)

    def att_section(g, s):
        d = ATT_DILATIONS[g]
        cos_ref, sin_ref = tab_refs[2 * g:2 * g + 2]
        o_ref = att_outs[3 * g + s]
        rows = PROJ_TM // d
        y = jnp.dot(h, wa_refs[s * ATT_GROUPS + g][...], preferred_element_type=F32)
        for hd in range(ATT_HEADS):
            cs = slice(hd * HEAD_DIM, (hd + 1) * HEAD_DIM)
            if d > 1:
                slab = slab_ref.at[(g - 1) * 3 + s, hd]
                slab[...] = y[:, cs]
                yh = jnp.concatenate([slab[pl.ds(r, rows, stride=d), :] for r in range(d)], axis=0)
            else:
                yh = y[:, cs]
            if s < 2:
                gain = (qn_ref if s == 0 else kn_ref)[g:g + 1, :]
                yh = _rope(_rms_rows(yh) * gain, cos_ref[...].reshape(PROJ_TM, HEAD_DIM),
                           sin_ref[...].reshape(PROJ_TM, HEAD_DIM))
                if s == 0:
                    yh = yh * scale
            o_ref[0, :, :, cs] = yh.astype(BF16).reshape(d, rows, HEAD_DIM)

    hg_fns = (_silu, None, None, _silu, jax.nn.sigmoid, jax.nn.sigmoid)

    def hg_section(s, c):
        lo = c * PROJ_NC
        if s < 4:
            w = wh_ref[:, s * HG_W + lo:s * HG_W + lo + PROJ_NC]
        else:
            w = wg_refs[(s - 4) * (D_MODEL // PROJ_NC) + c][...]
        y = jnp.dot(h, w, preferred_element_type=F32)
        if hg_fns[s] is not None:
            y = hg_fns[s](y)
        hg_outs[s][:, lo:lo + PROJ_NC] = y.astype(hg_outs[s].dtype)

    att = [(g, s) for g in reversed(range(ATT_GROUPS)) for s in range(3)]
    hgs = [(s, c) for s in range(len(hg_fns)) for c in range(HG_W // PROJ_NC)]
    while att or hgs:
        if att:
            att_section(*att.pop(0))
        for _ in range(2 if len(hgs) >= 2 * len(att) or len(att) > 5 else 1):
            if hgs:
                hg_section(*hgs.pop(0))


def _proj(x3d, gain, w_in, layer, qn, kn, tables):
    b, t, _ = x3d.shape
    nt = t // PROJ_TM
    in_specs = [
        pl.BlockSpec((None, PROJ_TM, D_MODEL), lambda bi, i: (bi, i, 0)),
        _resident((1, D_MODEL)),
        _resident((ATT_GROUPS, HEAD_DIM)),
        _resident((ATT_GROUPS, HEAD_DIM)),
        _layer_block((D_MODEL, 4 * HG_W), layer),
    ]
    in_specs += [_layer_block((D_MODEL, PROJ_NC), layer, COL_GA // PROJ_NC + c) for c in range(N_GATE_BLOCKS)]
    in_specs += [_layer_block((D_MODEL, ATT_GW), layer, COL_AQ // ATT_GW + k) for k in range(N_ATT_BLOCKS)]
    for d in ATT_DILATIONS:
        for _ in range(2):
            in_specs.append(pl.BlockSpec((d, PROJ_TM // d, HEAD_DIM), lambda bi, i: (0, i, 0)))
    tile = lambda: pl.BlockSpec((PROJ_TM, HG_W), lambda bi, i: (bi * nt + i, 0))
    bf = jax.ShapeDtypeStruct((b * t, HG_W), BF16)
    out_specs = [tile() for _ in range(6)]
    out_shape = [bf, jax.ShapeDtypeStruct((b * t, HG_W), F32), bf, bf, bf, bf]
    for d in ATT_DILATIONS:
        for _ in range(3):
            out_specs.append(pl.BlockSpec((1, d, PROJ_TM // d, ATT_GW), lambda bi, i: (bi, 0, i, 0)))
            out_shape.append(jax.ShapeDtypeStruct((b, d, t // d, ATT_GW), BF16))
    outs = pl.pallas_call(
        _proj_kernel,
        grid=(b, nt),
        in_specs=in_specs,
        out_specs=out_specs,
        out_shape=out_shape,
        scratch_shapes=[pltpu.VMEM((2 * 3, ATT_HEADS, PROJ_TM, HEAD_DIM), F32)],
        compiler_params=pltpu.CompilerParams(
            dimension_semantics=("parallel", "parallel"), vmem_limit_bytes=VMEM_LIMIT_BYTES_V7X),
        name="proj",
    )(x3d, gain.reshape(1, D_MODEL), qn, kn, w_in, *([w_in] * (N_GATE_BLOCKS + N_ATT_BLOCKS)), *tables)
    return outs[:6], outs[6:]


def _rope_tables(t):
    pos = np.arange(t, dtype=np.float64)
    inv = ROPE_THETA ** (-np.arange(0, HEAD_DIM, 2, dtype=np.float64) / HEAD_DIM)
    ang = pos[:, None] * inv[None, :]
    cos = np.concatenate([np.cos(ang), np.cos(ang)], axis=-1).astype(np.float32)
    sin = np.concatenate([-np.sin(ang), np.sin(ang)], axis=-1).astype(np.float32)
    tables = []
    for d in ATT_DILATIONS:
        for tab in (cos, sin):
            tables.append(jnp.asarray(np.ascontiguousarray(tab.reshape(t // d, d, HEAD_DIM).transpose(1, 0, 2))))
    return tables


def _hgrn_levels(c):
    levels = []
    m = 1
    while m < c:
        levels.append(m)
        m *= 2
    return levels


def _hgrn_kernel(layer, qa_ref, hf_ref, va_ref, gate_ref, lbl_ref, on_ref, o_ref, state_ref):
    c = HG_C

    @pl.when(pl.program_id(1) == 0)
    def _():
        state_ref[...] = jnp.zeros_like(state_ref)

    logits = lbl_ref[...]
    e = jnp.exp(logits - jnp.max(logits, axis=0, keepdims=True))
    sm = e / jnp.sum(e, axis=0, keepdims=True)
    lb_all = jnp.sum(sm[1:layer + 1], axis=0, keepdims=True) if layer > 0 else jnp.zeros((1, HG_W), F32)

    row = lax.broadcasted_iota(jnp.int32, (c, c), 0)
    colm = lax.broadcasted_iota(jnp.int32, (c, c), 1)
    tri = (colm <= row).astype(BF16)
    rix = lax.broadcasted_iota(jnp.int32, (c, HG_W), 0)
    sub = lax.broadcasted_iota(jnp.int32, (c // SUBLANES, SUBLANES, HG_W), 1)
    levels = _hgrn_levels(c)
    nrb = c // SUBLANES
    rblk = lambda a, i: a[i * SUBLANES:(i + 1) * SUBLANES]
    diag = [rblk(row == colm, i) for i in range(nrb)]
    sgns = {m: jnp.where(rix % (2 * m) >= m, 1.0, -1.0).astype(F32) for m in levels if 1 < m < BF16_ROWS}
    sels = {}
    for m in levels:
        sel = ((row // (2 * m)) == (colm // (2 * m))) & ((row % (2 * m)) >= m) & ((colm % (2 * m)) < m)
        sels[m] = [rblk(sel, i) for i in range(nrb)]
    odd_row = (rix % 2) == 1
    heads = [slice(hd * HEAD_DIM, (hd + 1) * HEAD_DIM) for hd in range(HG_HEADS)]
    nt_dims = (((1,), (1,)), ((), ()))

    def gates(ci):
        rows = pl.ds(ci * c, c)
        qb = qa_ref[rows, :]
        vb = va_ref[rows, :]
        f = lb_all + (1.0 - lb_all) * jax.nn.sigmoid(hf_ref[rows, :])
        k = 1.0 - f
        kb = k.astype(BF16)
        g = jnp.log2(f)
        g1 = g.astype(BF16)
        rem = g - g1.astype(F32)
        g2 = rem.astype(BF16)
        g3 = (rem - g2.astype(F32)).astype(BF16)
        gc = (jnp.dot(tri, g1, preferred_element_type=F32)
              + jnp.dot(tri, g2, preferred_element_type=F32)
              + jnp.dot(tri, g3, preferred_element_type=F32))
        g_last = gc[c - 1:c, :]
        qi = qb * jnp.exp2(gc).astype(BF16)
        ks = kb * jnp.exp2(g_last - gc).astype(BF16)
        decay = jnp.exp2(g_last)
        q1 = qb * jnp.where(odd_row, f, 1.0).astype(BF16)
        return rows, qb, vb, kb, gc, qi, ks, decay, q1

    def mix(rows, qb, vb, kb, gc, qi, ks, decay, q1):
        outs, accs = [], []
        for hd, cs in enumerate(heads):
            st = state_ref[hd]
            outs.append(lax.dot_general(qi[:, cs], st.astype(BF16), nt_dims, preferred_element_type=F32))
            upd = lax.dot_general(vb[:, cs], ks[:, cs], (((0,), (0,)), ((), ())), preferred_element_type=F32)
            state_ref[hd] = st * decay[:, cs] + upd
            p_d = lax.dot_general(jnp.concatenate([qb[:, cs], q1[:, cs]], axis=0), kb[:, cs], nt_dims,
                                  preferred_element_type=F32)
            accs.append([jnp.where(diag[i], rblk(p_d, i), jnp.where(sels[1][i], rblk(p_d, nrb + i), 0.0))
                         for i in range(nrb)])
        for m in levels[1:]:
            if m >= BF16_ROWS:
                kparts, qparts, qblocks = [], [], []
                for p in range(c // (2 * m)):
                    lo, mid, hi = p * 2 * m, p * 2 * m + m, (p + 1) * 2 * m
                    gb = gc[mid - 1:mid, :]
                    kparts.append(kb[lo:mid] * jnp.exp2(gb - gc[lo:mid]).astype(BF16))
                    kparts.append(kb[mid:hi])
                    qparts.append(qb[mid:hi] * jnp.exp2(gc[mid:hi] - gb).astype(BF16))
                    qblocks += list(range(mid // SUBLANES, hi // SUBLANES))
                kl = jnp.concatenate(kparts[:-1], axis=0)
                ql = jnp.concatenate(qparts, axis=0)
            else:
                qblocks = list(range(nrb))
                grp = max(SUBLANES, 2 * m)
                g3d = gc.reshape(c // grp, grp, HG_W)
                bnd3 = None
                for p in range(grp // (2 * m)):
                    src = jnp.broadcast_to(g3d[:, p * 2 * m + m - 1:p * 2 * m + m, :], (c // grp, grp, HG_W))
                    bnd3 = src if bnd3 is None else jnp.where(sub // (2 * m) == p, src, bnd3)
                ex = jnp.exp2((gc - bnd3.reshape(c, HG_W)) * sgns[m]).astype(BF16)
                ql = qb * ex
                kl = kb * ex
            for hd, cs in enumerate(heads):
                p_l = lax.dot_general(ql[:, cs], kl[:, cs], nt_dims, preferred_element_type=F32)
                n_keys = kl.shape[0]
                for n, i in enumerate(qblocks):
                    blk = rblk(p_l, n)
                    if n_keys < c:
                        blk = jnp.concatenate([blk, jnp.zeros((SUBLANES, c - n_keys), F32)], axis=1)
                    accs[hd][i] = jnp.where(sels[m][i], blk, accs[hd][i])
        return rows, vb, outs, accs

    def emit(rows, vb, outs, accs):
        for hd, cs in enumerate(heads):
            a = jnp.concatenate(accs[hd], axis=0).astype(BF16)
            o = outs[hd] + jnp.dot(a, vb[:, cs], preferred_element_type=F32)
            o = _rms_rows(o) * on_ref[:, cs] * gate_ref[rows, cs].astype(F32)
            o_ref[rows, cs] = o.astype(BF16)

    n_chunks = HG_TB // c
    staged, pending = gates(0), None
    for ci in range(n_chunks):
        upcoming = gates(ci + 1) if ci + 1 < n_chunks else None
        mixed = mix(*staged)
        staged = upcoming
        if pending is not None:
            emit(*pending)
        pending = mixed
    emit(*pending)


def _hgrn(layer, qa, hf, va, gate, lb_logits, out_norm, b, t):
    nt = t // HG_TB
    tile = lambda: pl.BlockSpec((HG_TB, HG_W), lambda bi, i: (bi * nt + i, 0))
    return pl.pallas_call(
        functools.partial(_hgrn_kernel, layer),
        grid=(b, nt),
        in_specs=[tile(), tile(), tile(), tile(), _resident((DEPTH, HG_W)), _resident((1, HG_W))],
        out_specs=tile(),
        out_shape=jax.ShapeDtypeStruct((b * t, HG_W), BF16),
        scratch_shapes=[pltpu.VMEM((HG_HEADS, HEAD_DIM, HEAD_DIM), F32)],
        compiler_params=pltpu.CompilerParams(
            dimension_semantics=("parallel", "arbitrary"), vmem_limit_bytes=VMEM_LIMIT_BYTES_V7X),
        name="hgrn",
    )(qa, hf, va, gate, lb_logits, out_norm.reshape(1, HG_W))


def _attn_kernel(*refs):
    ins, o_ref, osc_ref, lsc_ref = refs[:15], refs[15], refs[16], refs[17]
    qi = lax.broadcasted_iota(jnp.int32, (ATT_BACK, 2 * ATT_BACK), 0)
    ki = lax.broadcasted_iota(jnp.int32, (ATT_BACK, 2 * ATT_BACK), 1)
    band = (ki >= qi) & (ki <= qi + ATT_BACK)
    first_key = jnp.where(pl.program_id(2) == 0, ATT_BACK, 0)
    band_first = band & (ki >= first_key)

    def head_blocks(hh):
        hs = slice(hh * HEAD_DIM, (hh + 1) * HEAD_DIM)
        for g, d in enumerate(ATT_DILATIONS):
            q_ref, kc_ref, kp_ref, vc_ref, vp_ref = ins[5 * g:5 * g + 5]
            nblk = ATT_TT // d // ATT_BACK
            for r in range(d):
                for j in range(nblk):
                    q = q_ref[r, j * ATT_BACK:(j + 1) * ATT_BACK, hs]
                    if j == 0:
                        kk = jnp.concatenate([kp_ref[r, :, hs], kc_ref[r, 0:ATT_BACK, hs]], axis=0)
                        vv = jnp.concatenate([vp_ref[r, :, hs], vc_ref[r, 0:ATT_BACK, hs]], axis=0)
                        mask = band_first
                    else:
                        kk = kc_ref[r, (j - 1) * ATT_BACK:(j + 1) * ATT_BACK, hs]
                        vv = vc_ref[r, (j - 1) * ATT_BACK:(j + 1) * ATT_BACK, hs]
                        mask = band
                    s = lax.dot_general(q, kk, (((1,), (1,)), ((), ())), preferred_element_type=F32)
                    s = jnp.where(mask, s, -jnp.inf)
                    mx = jnp.max(s, axis=-1, keepdims=True)
                    p = jnp.exp2(s - mx)
                    den = jnp.sum(p, axis=-1, keepdims=True)
                    o = jnp.dot(p.astype(BF16), vv, preferred_element_type=F32) / den
                    lse = mx + jnp.log2(den)
                    dst = (pl.ds(j * ATT_BACK * d + r, ATT_BACK, stride=d) if d > 1
                           else pl.ds(j * ATT_BACK, ATT_BACK))
                    osc_ref[hh, g, dst, :] = o
                    lsc_ref[hh, g, dst, :] = jnp.broadcast_to(lse, (ATT_BACK, HEAD_DIM))

    def head_merge(hh):
        for j in range(ATT_TT // ATT_BACK):
            rs = slice(j * ATT_BACK, (j + 1) * ATT_BACK)
            l0, l1, l2 = lsc_ref[hh, 0, rs, :], lsc_ref[hh, 1, rs, :], lsc_ref[hh, 2, rs, :]
            mx = jnp.maximum(jnp.maximum(l0, l1), l2)
            w0, w1, w2 = jnp.exp2(l0 - mx), jnp.exp2(l1 - mx), jnp.exp2(l2 - mx)
            o = (w0 * osc_ref[hh, 0, rs, :] + w1 * osc_ref[hh, 1, rs, :] + w2 * osc_ref[hh, 2, rs, :]) / (w0 + w1 + w2)
            o_ref[rs, hh * HEAD_DIM:(hh + 1) * HEAD_DIM] = o.astype(BF16)

    for hh in range(ATT_HPS):
        head_blocks(hh)
        if hh > 0:
            head_merge(hh - 1)
    head_merge(ATT_HPS - 1)


def _attn(qkv, b, t):
    nt = t // ATT_TT
    in_specs, args = [], []
    for g, d in enumerate(ATT_DILATIONS):
        rows = ATT_TT // d
        nb = rows // ATT_BACK
        cur = lambda rows=rows, d=d: pl.BlockSpec((None, d, rows, ATT_HPS * HEAD_DIM), lambda bi, h, i: (bi, 0, i, h))
        prev = lambda nb=nb, d=d: pl.BlockSpec(
            (None, d, ATT_BACK, ATT_HPS * HEAD_DIM), lambda bi, h, i: (bi, 0, jnp.maximum(i * nb - 1, 0), h))
        q, k, v = qkv[g]
        in_specs += [cur(), cur(), prev(), cur(), prev()]
        args += [q, k, k, v, v]
    return pl.pallas_call(
        _attn_kernel,
        grid=(b, ATT_HEADS // ATT_HPS, nt),
        in_specs=in_specs,
        out_specs=pl.BlockSpec((None, ATT_TT, ATT_HPS * HEAD_DIM), lambda bi, h, i: (bi, i, h)),
        out_shape=jax.ShapeDtypeStruct((b, t, ATT_GW), BF16),
        scratch_shapes=[pltpu.VMEM((ATT_HPS, ATT_GROUPS, ATT_TT, HEAD_DIM), F32),
                        pltpu.VMEM((ATT_HPS, ATT_GROUPS, ATT_TT, HEAD_DIM), F32)],
        compiler_params=pltpu.CompilerParams(
            dimension_semantics=("parallel", "parallel", "arbitrary"), vmem_limit_bytes=VMEM_LIMIT_BYTES_V7X),
        name="attn",
    )(*args)


def _merge_kernel(x_ref, oa_ref, ob_ref, sga_ref, sgb_ref, wa_ref, wb_ref, wo_ref, o_ref):
    ya = jnp.dot(oa_ref[...], wa_ref[...], preferred_element_type=F32)
    yb = jnp.dot(ob_ref[...], wb_ref[...], preferred_element_type=F32)
    merged = sga_ref[...].astype(F32) * ya + sgb_ref[...].astype(F32) * yb
    o_ref[...] = x_ref[...] + jnp.dot(merged.astype(BF16), wo_ref[...], preferred_element_type=F32)


def _merge(x2d, oa, ob, sga, sgb, wa, wb, wo, layer):
    n = x2d.shape[0]
    tile = lambda w: pl.BlockSpec((MERGE_TM, w), lambda i: (i, 0))
    return pl.pallas_call(
        _merge_kernel,
        grid=(n // MERGE_TM,),
        in_specs=[tile(D_MODEL), tile(HG_W), tile(ATT_GW), tile(D_MODEL), tile(D_MODEL),
                  _layer_block((HG_W, D_MODEL), layer), _layer_block((ATT_GW, D_MODEL), layer),
                  _layer_block((D_MODEL, D_MODEL), layer)],
        out_specs=tile(D_MODEL),
        out_shape=jax.ShapeDtypeStruct((n, D_MODEL), F32),
        compiler_params=pltpu.CompilerParams(
            dimension_semantics=("parallel",), vmem_limit_bytes=VMEM_LIMIT_BYTES_V7X),
        name="merge",
    )(x2d, oa, ob, sga, sgb, wa, wb, wo)


def kernel(x, ffn1_norm, ffn1_w_in, ffn1_w_out, mix_norm, w_in, hgrn_lb_logits, hgrn_out_norm, attn_q_norm,
           attn_k_norm, w_branch_a, w_branch_b, w_out, ffn2_norm, ffn2_w_in, ffn2_w_out):
    b, t, _ = x.shape
    assert x.shape[2] == D_MODEL and t % ATT_TT == 0 and t % HG_TB == 0
    assert COL_GA % PROJ_NC == 0 and COL_AQ % ATT_GW == 0
    tables = _rope_tables(t)
    ffn1_w_in, ffn1_w_out, w_in, w_branch_a, w_branch_b, w_out, ffn2_w_in, ffn2_w_out = (
        w.astype(BF16) for w in (ffn1_w_in, ffn1_w_out, w_in, w_branch_a, w_branch_b, w_out, ffn2_w_in, ffn2_w_out))
    x2d = x.reshape(b * t, D_MODEL)
    for l in range(DEPTH):
        x2d = _ffn(x2d, ffn1_norm[l], ffn1_w_in, ffn1_w_out, l)

        (qa, hf, va, gate, sga, sgb), qkv = _proj(
            x2d.reshape(b, t, D_MODEL), mix_norm[l], w_in, l, attn_q_norm[l], attn_k_norm[l], tables)
        oa = _hgrn(l, qa, hf, va, gate, hgrn_lb_logits, hgrn_out_norm[l], b, t)
        ob = _attn([qkv[3 * g:3 * g + 3] for g in range(ATT_GROUPS)], b, t)
        x2d = _merge(x2d, oa, ob.reshape(b * t, ATT_GW), sga, sgb, w_branch_a, w_branch_b, w_out, l)

        x2d = _ffn(x2d, ffn2_norm[l], ffn2_w_in, ffn2_w_out, l)
    return x2d.reshape(b, t, D_MODEL)
```

```python
import functools
import math

import jax
import jax.numpy as jnp
import numpy as np
from jax import lax
from jax.experimental import pallas as pl
from jax.experimental.pallas import tpu as pltpu

F32 = jnp.float32
BF16 = jnp.bfloat16

D_MODEL = 1024
DEPTH = 2
D_FF = 2816
HEAD_DIM = 128
HG_HEADS = D_MODEL // HEAD_DIM
HG_W = HG_HEADS * HEAD_DIM
ATT_DILATIONS = (1, 4, 16)
ATT_BACK = 128
ATT_GROUPS = 3
ATT_HEADS = 4
ATT_GW = ATT_HEADS * HEAD_DIM
ATT_W = ATT_GROUPS * ATT_GW
ROPE_THETA = 10000.0
EPS = 1e-6

COL_HQ, COL_HF, COL_HI, COL_HG = 0, HG_W, 2 * HG_W, 3 * HG_W
COL_AQ = 4 * HG_W
COL_AK = COL_AQ + ATT_W
COL_AV = COL_AK + ATT_W
COL_GA = COL_AV + ATT_W
COL_GB = COL_GA + D_MODEL
P_IN = COL_GB + D_MODEL

VMEM_LIMIT_BYTES_V7X = 56 * 1024 * 1024
SUBLANES = 8
BF16_ROWS = 16

FFN_TM = 1024
FFN_FC = 256
PROJ_TM = 256
PROJ_NC = 512
HG_TB = 512
HG_C = 128
ATT_TT = ATT_BACK * ATT_DILATIONS[-1]
ATT_HPS = 2
MERGE_TM = 1024


def _resident(shape):
    nd = len(shape)
    return pl.BlockSpec(shape, lambda *_: (0,) * nd, pipeline_mode=pl.Buffered(1))


def _layer_block(shape, layer, col_block=0):
    return pl.BlockSpec((None,) + tuple(shape), lambda *_: (layer, 0, col_block), pipeline_mode=pl.Buffered(1))


def _rms_rows(x):
    return x * lax.rsqrt(jnp.mean(x * x, axis=-1, keepdims=True) + EPS)


def _silu(x):
    return x * jax.nn.sigmoid(x)


def _ffn_kernel(x_ref, g_ref, win_ref, wout_ref, o_ref, act_ref):
    x = x_ref[...]
    h = (_rms_rows(x) * g_ref[...]).astype(BF16)
    for c in range(D_FF // FFN_FC):
        lo = c * FFN_FC
        a = jnp.dot(h, win_ref[:, lo:lo + FFN_FC], preferred_element_type=F32)
        b = jnp.dot(h, win_ref[:, D_FF + lo:D_FF + lo + FFN_FC], preferred_element_type=F32)
        act_ref[:, lo:lo + FFN_FC] = (_silu(a) * b).astype(BF16)
    y = jnp.dot(act_ref[...], wout_ref[...], preferred_element_type=F32)
    o_ref[...] = x + 0.5 * y


def _ffn(x2d, gain, w_in, w_out, layer):
    n = x2d.shape[0]
    return pl.pallas_call(
        _ffn_kernel,
        grid=(n // FFN_TM,),
        in_specs=[
            pl.BlockSpec((FFN_TM, D_MODEL), lambda i: (i, 0)),
            _resident((1, D_MODEL)),
            _layer_block((D_MODEL, 2 * D_FF), layer),
            _layer_block((D_FF, D_MODEL), layer),
        ],
        out_specs=pl.BlockSpec((FFN_TM, D_MODEL), lambda i: (i, 0)),
        out_shape=jax.ShapeDtypeStruct((n, D_MODEL), F32),
        scratch_shapes=[pltpu.VMEM((FFN_TM, D_FF), BF16)],
        compiler_params=pltpu.CompilerParams(
            dimension_semantics=("parallel",), vmem_limit_bytes=VMEM_LIMIT_BYTES_V7X),
        name="ffn",
    )(x2d, gain.reshape(1, D_MODEL), w_in, w_out)


def _rope(y, cos, sin_signed):
    return y * cos + pltpu.roll(y, HEAD_DIM // 2, axis=1) * sin_signed


N_GATE_BLOCKS = 2 * D_MODEL // PROJ_NC
N_ATT_BLOCKS = 3 * ATT_GROUPS


def _proj_kernel(x_ref, g_ref, qn_ref, kn_ref, wh_ref, *refs):
    wg_refs, refs = refs[:N_GATE_BLOCKS], refs[N_GATE_BLOCKS:]
    wa_refs, refs = refs[:N_ATT_BLOCKS], refs[N_ATT_BLOCKS:]
    tab_refs, hg_outs, att_outs, slab_ref = refs[:6], refs[6:12], refs[12:21], refs[21]
    h = (_rms_rows(x_ref[...]) * g_ref[...]).astype(BF16)
    scale = HEAD_DIM ** -0.5 * math.log2(math.e)

    def att_section(g, s):
        d = ATT_DILATIONS[g]
        cos_ref, sin_ref = tab_refs[2 * g:2 * g + 2]
        o_ref = att_outs[3 * g + s]
        rows = PROJ_TM // d
        y = jnp.dot(h, wa_refs[s * ATT_GROUPS + g][...], preferred_element_type=F32)
        for hd in range(ATT_HEADS):
            cs = slice(hd * HEAD_DIM, (hd + 1) * HEAD_DIM)
            if d > 1:
                slab = slab_ref.at[(g - 1) * 3 + s, hd]
                slab[...] = y[:, cs]
                yh = jnp.concatenate([slab[pl.ds(r, rows, stride=d), :] for r in range(d)], axis=0)
            else:
                yh = y[:, cs]
            if s < 2:
                gain = (qn_ref if s == 0 else kn_ref)[g:g + 1, :]
                yh = _rope(_rms_rows(yh) * gain, cos_ref[...].reshape(PROJ_TM, HEAD_DIM),
                           sin_ref[...].reshape(PROJ_TM, HEAD_DIM))
                if s == 0:
                    yh = yh * scale
            o_ref[0, :, :, cs] = yh.astype(BF16).reshape(d, rows, HEAD_DIM)

    hg_fns = (_silu, None, None, _silu, jax.nn.sigmoid, jax.nn.sigmoid)

    def hg_section(s, c):
        lo = c * PROJ_NC
        if s < 4:
            w = wh_ref[:, s * HG_W + lo:s * HG_W + lo + PROJ_NC]
        else:
            w = wg_refs[(s - 4) * (D_MODEL // PROJ_NC) + c][...]
        y = jnp.dot(h, w, preferred_element_type=F32)
        if hg_fns[s] is not None:
            y = hg_fns[s](y)
        hg_outs[s][:, lo:lo + PROJ_NC] = y.astype(hg_outs[s].dtype)

    att = [(g, s) for g in reversed(range(ATT_GROUPS)) for s in range(3)]
    hgs = [(s, c) for s in range(len(hg_fns)) for c in range(HG_W // PROJ_NC)]
    while att or hgs:
        if att:
            att_section(*att.pop(0))
        for _ in range(2 if len(hgs) >= 2 * len(att) or len(att) > 5 else 1):
            if hgs:
                hg_section(*hgs.pop(0))


def _proj(x3d, gain, w_in, layer, qn, kn, tables):
    b, t, _ = x3d.shape
    nt = t // PROJ_TM
    in_specs = [
        pl.BlockSpec((None, PROJ_TM, D_MODEL), lambda bi, i: (bi, i, 0)),
        _resident((1, D_MODEL)),
        _resident((ATT_GROUPS, HEAD_DIM)),
        _resident((ATT_GROUPS, HEAD_DIM)),
        _layer_block((D_MODEL, 4 * HG_W), layer),
    ]
    in_specs += [_layer_block((D_MODEL, PROJ_NC), layer, COL_GA // PROJ_NC + c) for c in range(N_GATE_BLOCKS)]
    in_specs += [_layer_block((D_MODEL, ATT_GW), layer, COL_AQ // ATT_GW + k) for k in range(N_ATT_BLOCKS)]
    for d in ATT_DILATIONS:
        for _ in range(2):
            in_specs.append(pl.BlockSpec((d, PROJ_TM // d, HEAD_DIM), lambda bi, i: (0, i, 0)))
    tile = lambda: pl.BlockSpec((PROJ_TM, HG_W), lambda bi, i: (bi * nt + i, 0))
    bf = jax.ShapeDtypeStruct((b * t, HG_W), BF16)
    out_specs = [tile() for _ in range(6)]
    out_shape = [bf, jax.ShapeDtypeStruct((b * t, HG_W), F32), bf, bf, bf, bf]
    for d in ATT_DILATIONS:
        for _ in range(3):
            out_specs.append(pl.BlockSpec((1, d, PROJ_TM // d, ATT_GW), lambda bi, i: (bi, 0, i, 0)))
            out_shape.append(jax.ShapeDtypeStruct((b, d, t // d, ATT_GW), BF16))
    outs = pl.pallas_call(
        _proj_kernel,
        grid=(b, nt),
        in_specs=in_specs,
        out_specs=out_specs,
        out_shape=out_shape,
        scratch_shapes=[pltpu.VMEM((2 * 3, ATT_HEADS, PROJ_TM, HEAD_DIM), F32)],
        compiler_params=pltpu.CompilerParams(
            dimension_semantics=("parallel", "parallel"), vmem_limit_bytes=VMEM_LIMIT_BYTES_V7X),
        name="proj",
    )(x3d, gain.reshape(1, D_MODEL), qn, kn, w_in, *([w_in] * (N_GATE_BLOCKS + N_ATT_BLOCKS)), *tables)
    return outs[:6], outs[6:]


def _rope_tables(t):
    pos = np.arange(t, dtype=np.float64)
    inv = ROPE_THETA ** (-np.arange(0, HEAD_DIM, 2, dtype=np.float64) / HEAD_DIM)
    ang = pos[:, None] * inv[None, :]
    cos = np.concatenate([np.cos(ang), np.cos(ang)], axis=-1).astype(np.float32)
    sin = np.concatenate([-np.sin(ang), np.sin(ang)], axis=-1).astype(np.float32)
    tables = []
    for d in ATT_DILATIONS:
        for tab in (cos, sin):
            tables.append(jnp.asarray(np.ascontiguousarray(tab.reshape(t // d, d, HEAD_DIM).transpose(1, 0, 2))))
    return tables


def _hgrn_levels(c):
    levels = []
    m = 1
    while m < c:
        levels.append(m)
        m *= 2
    return levels


def _hgrn_kernel(layer, qa_ref, hf_ref, va_ref, gate_ref, lbl_ref, on_ref, x_ref, ob_ref, sga_ref, sgb_ref,
                 wa_ref, wb_ref, wo_ref, xo_ref, state_ref, o_ref):
    c = HG_C

    @pl.when(pl.program_id(1) == 0)
    def _():
        state_ref[...] = jnp.zeros_like(state_ref)

    logits = lbl_ref[...]
    e = jnp.exp(logits - jnp.max(logits, axis=0, keepdims=True))
    sm = e / jnp.sum(e, axis=0, keepdims=True)
    lb_all = jnp.sum(sm[1:layer + 1], axis=0, keepdims=True) if layer > 0 else jnp.zeros((1, HG_W), F32)

    row = lax.broadcasted_iota(jnp.int32, (c, c), 0)
    colm = lax.broadcasted_iota(jnp.int32, (c, c), 1)
    tri = (colm <= row).astype(BF16)
    rix = lax.broadcasted_iota(jnp.int32, (c, HG_W), 0)
    sub = lax.broadcasted_iota(jnp.int32, (c // SUBLANES, SUBLANES, HG_W), 1)
    levels = _hgrn_levels(c)
    nrb = c // SUBLANES
    rblk = lambda a, i: a[i * SUBLANES:(i + 1) * SUBLANES]
    diag = [rblk(row == colm, i) for i in range(nrb)]
    sgns = {m: jnp.where(rix % (2 * m) >= m, 1.0, -1.0).astype(F32) for m in levels if 1 < m < BF16_ROWS}
    sels = {}
    for m in levels:
        sel = ((row // (2 * m)) == (colm // (2 * m))) & ((row % (2 * m)) >= m) & ((colm % (2 * m)) < m)
        sels[m] = [rblk(sel, i) for i in range(nrb)]
    odd_row = (rix % 2) == 1
    heads = [slice(hd * HEAD_DIM, (hd + 1) * HEAD_DIM) for hd in range(HG_HEADS)]
    nt_dims = (((1,), (1,)), ((), ()))

    def gates(ci):
        rows = pl.ds(ci * c, c)
        qb = qa_ref[rows, :]
        vb = va_ref[rows, :]
        f = lb_all + (1.0 - lb_all) * jax.nn.sigmoid(hf_ref[rows, :])
        k = 1.0 - f
        kb = k.astype(BF16)
        g = jnp.log2(f)
        g1 = g.astype(BF16)
        rem = g - g1.astype(F32)
        g2 = rem.astype(BF16)
        g3 = (rem - g2.astype(F32)).astype(BF16)
        gc = (jnp.dot(tri, g1, preferred_element_type=F32)
              + jnp.dot(tri, g2, preferred_element_type=F32)
              + jnp.dot(tri, g3, preferred_element_type=F32))
        g_last = gc[c - 1:c, :]
        qi = qb * jnp.exp2(gc).astype(BF16)
        ks = kb * jnp.exp2(g_last - gc).astype(BF16)
        decay = jnp.exp2(g_last)
        q1 = qb * jnp.where(odd_row, f, 1.0).astype(BF16)
        return rows, qb, vb, kb, gc, qi, ks, decay, q1

    def mix(rows, qb, vb, kb, gc, qi, ks, decay, q1):
        outs, accs = [], []
        for hd, cs in enumerate(heads):
            st = state_ref[hd]
            outs.append(lax.dot_general(qi[:, cs], st.astype(BF16), nt_dims, preferred_element_type=F32))
            upd = lax.dot_general(vb[:, cs], ks[:, cs], (((0,), (0,)), ((), ())), preferred_element_type=F32)
            state_ref[hd] = st * decay[:, cs] + upd
            p_d = lax.dot_general(jnp.concatenate([qb[:, cs], q1[:, cs]], axis=0), kb[:, cs], nt_dims,
                                  preferred_element_type=F32)
            accs.append([jnp.where(diag[i], rblk(p_d, i), jnp.where(sels[1][i], rblk(p_d, nrb + i), 0.0))
                         for i in range(nrb)])
        for m in levels[1:]:
            if m >= BF16_ROWS:
                kparts, qparts, qblocks = [], [], []
                for p in range(c // (2 * m)):
                    lo, mid, hi = p * 2 * m, p * 2 * m + m, (p + 1) * 2 * m
                    gb = gc[mid - 1:mid, :]
                    kparts.append(kb[lo:mid] * jnp.exp2(gb - gc[lo:mid]).astype(BF16))
                    kparts.append(kb[mid:hi])
                    qparts.append(qb[mid:hi] * jnp.exp2(gc[mid:hi] - gb).astype(BF16))
                    qblocks += list(range(mid // SUBLANES, hi // SUBLANES))
                kl = jnp.concatenate(kparts[:-1], axis=0)
                ql = jnp.concatenate(qparts, axis=0)
            else:
                qblocks = list(range(nrb))
                grp = max(SUBLANES, 2 * m)
                g3d = gc.reshape(c // grp, grp, HG_W)
                bnd3 = None
                for p in range(grp // (2 * m)):
                    src = jnp.broadcast_to(g3d[:, p * 2 * m + m - 1:p * 2 * m + m, :], (c // grp, grp, HG_W))
                    bnd3 = src if bnd3 is None else jnp.where(sub // (2 * m) == p, src, bnd3)
                ex = jnp.exp2((gc - bnd3.reshape(c, HG_W)) * sgns[m]).astype(BF16)
                ql = qb * ex
                kl = kb * ex
            for hd, cs in enumerate(heads):
                p_l = lax.dot_general(ql[:, cs], kl[:, cs], nt_dims, preferred_element_type=F32)
                n_keys = kl.shape[0]
                for n, i in enumerate(qblocks):
                    blk = rblk(p_l, n)
                    if n_keys < c:
                        blk = jnp.concatenate([blk, jnp.zeros((SUBLANES, c - n_keys), F32)], axis=1)
                    accs[hd][i] = jnp.where(sels[m][i], blk, accs[hd][i])
        return rows, vb, outs, accs

    def emit(rows, vb, outs, accs):
        for hd, cs in enumerate(heads):
            a = jnp.concatenate(accs[hd], axis=0).astype(BF16)
            o = outs[hd] + jnp.dot(a, vb[:, cs], preferred_element_type=F32)
            o = _rms_rows(o) * on_ref[:, cs] * gate_ref[rows, cs].astype(F32)
            o_ref[rows, cs] = o.astype(BF16)

    def merge_rows(j):
        rs = slice(j * 2 * c, (j + 1) * 2 * c)
        ya = jnp.dot(o_ref[rs, :], wa_ref[...], preferred_element_type=F32)
        yb = jnp.dot(ob_ref[rs, :], wb_ref[...], preferred_element_type=F32)
        merged = sga_ref[rs, :].astype(F32) * ya + sgb_ref[rs, :].astype(F32) * yb
        xo_ref[rs, :] = x_ref[rs, :] + jnp.dot(merged.astype(BF16), wo_ref[...], preferred_element_type=F32)

    n_chunks = HG_TB // c
    staged, pending = gates(0), None
    for ci in range(n_chunks):
        upcoming = gates(ci + 1) if ci + 1 < n_chunks else None
        mixed = mix(*staged)
        staged = upcoming
        if pending is not None:
            emit(*pending)
            if ci % 2 == 0:
                merge_rows(ci // 2 - 1)
        pending = mixed
    emit(*pending)
    merge_rows(n_chunks // 2 - 1)


def _hgrn(layer, qa, hf, va, gate, lb_logits, out_norm, x2d, ob, sga, sgb, wa, wb, wo, b, t):
    nt = t // HG_TB
    tile = lambda w=HG_W: pl.BlockSpec((HG_TB, w), lambda bi, i: (bi * nt + i, 0))
    return pl.pallas_call(
        functools.partial(_hgrn_kernel, layer),
        grid=(b, nt),
        in_specs=[tile(), tile(), tile(), tile(), _resident((DEPTH, HG_W)), _resident((1, HG_W)),
                  tile(D_MODEL), tile(ATT_GW), tile(D_MODEL), tile(D_MODEL),
                  _layer_block((HG_W, D_MODEL), layer), _layer_block((ATT_GW, D_MODEL), layer),
                  _layer_block((D_MODEL, D_MODEL), layer)],
        out_specs=tile(D_MODEL),
        out_shape=jax.ShapeDtypeStruct((b * t, D_MODEL), F32),
        scratch_shapes=[pltpu.VMEM((HG_HEADS, HEAD_DIM, HEAD_DIM), F32), pltpu.VMEM((HG_TB, HG_W), BF16)],
        compiler_params=pltpu.CompilerParams(
            dimension_semantics=("parallel", "arbitrary"), vmem_limit_bytes=VMEM_LIMIT_BYTES_V7X),
        name="hgrn",
    )(qa, hf, va, gate, lb_logits, out_norm.reshape(1, HG_W), x2d, ob, sga, sgb, wa, wb, wo)


def _attn_kernel(*refs):
    ins, o_ref, osc_ref, lsc_ref = refs[:15], refs[15], refs[16], refs[17]
    qi = lax.broadcasted_iota(jnp.int32, (ATT_BACK, 2 * ATT_BACK), 0)
    ki = lax.broadcasted_iota(jnp.int32, (ATT_BACK, 2 * ATT_BACK), 1)
    band = (ki >= qi) & (ki <= qi + ATT_BACK)
    first_key = jnp.where(pl.program_id(2) == 0, ATT_BACK, 0)
    band_first = band & (ki >= first_key)

    def head_blocks(hh):
        hs = slice(hh * HEAD_DIM, (hh + 1) * HEAD_DIM)
        for g, d in enumerate(ATT_DILATIONS):
            q_ref, kc_ref, kp_ref, vc_ref, vp_ref = ins[5 * g:5 * g + 5]
            nblk = ATT_TT // d // ATT_BACK
            for r in range(d):
                for j in range(nblk):
                    q = q_ref[r, j * ATT_BACK:(j + 1) * ATT_BACK, hs]
                    if j == 0:
                        kk = jnp.concatenate([kp_ref[r, :, hs], kc_ref[r, 0:ATT_BACK, hs]], axis=0)
                        vv = jnp.concatenate([vp_ref[r, :, hs], vc_ref[r, 0:ATT_BACK, hs]], axis=0)
                        mask = band_first
                    else:
                        kk = kc_ref[r, (j - 1) * ATT_BACK:(j + 1) * ATT_BACK, hs]
                        vv = vc_ref[r, (j - 1) * ATT_BACK:(j + 1) * ATT_BACK, hs]
                        mask = band
                    s = lax.dot_general(q, kk, (((1,), (1,)), ((), ())), preferred_element_type=F32)
                    s = jnp.where(mask, s, -jnp.inf)
                    mx = jnp.max(s, axis=-1, keepdims=True)
                    p = jnp.exp2(s - mx)
                    den = jnp.sum(p, axis=-1, keepdims=True)
                    o = jnp.dot(p.astype(BF16), vv, preferred_element_type=F32) / den
                    lse = mx + jnp.log2(den)
                    dst = (pl.ds(j * ATT_BACK * d + r, ATT_BACK, stride=d) if d > 1
                           else pl.ds(j * ATT_BACK, ATT_BACK))
                    osc_ref[hh, g, dst, :] = o
                    lsc_ref[hh, g, dst, :] = jnp.broadcast_to(lse, (ATT_BACK, HEAD_DIM))

    def head_merge(hh):
        for j in range(ATT_TT // ATT_BACK):
            rs = slice(j * ATT_BACK, (j + 1) * ATT_BACK)
            l0, l1, l2 = lsc_ref[hh, 0, rs, :], lsc_ref[hh, 1, rs, :], lsc_ref[hh, 2, rs, :]
            mx = jnp.maximum(jnp.maximum(l0, l1), l2)
            w0, w1, w2 = jnp.exp2(l0 - mx), jnp.exp2(l1 - mx), jnp.exp2(l2 - mx)
            o = (w0 * osc_ref[hh, 0, rs, :] + w1 * osc_ref[hh, 1, rs, :] + w2 * osc_ref[hh, 2, rs, :]) / (w0 + w1 + w2)
            o_ref[rs, hh * HEAD_DIM:(hh + 1) * HEAD_DIM] = o.astype(BF16)

    for hh in range(ATT_HPS):
        head_blocks(hh)
        if hh > 0:
            head_merge(hh - 1)
    head_merge(ATT_HPS - 1)


def _attn(qkv, b, t):
    nt = t // ATT_TT
    in_specs, args = [], []
    for g, d in enumerate(ATT_DILATIONS):
        rows = ATT_TT // d
        nb = rows // ATT_BACK
        cur = lambda rows=rows, d=d: pl.BlockSpec((None, d, rows, ATT_HPS * HEAD_DIM), lambda bi, h, i: (bi, 0, i, h))
        prev = lambda nb=nb, d=d: pl.BlockSpec(
            (None, d, ATT_BACK, ATT_HPS * HEAD_DIM), lambda bi, h, i: (bi, 0, jnp.maximum(i * nb - 1, 0), h))
        q, k, v = qkv[g]
        in_specs += [cur(), cur(), prev(), cur(), prev()]
        args += [q, k, k, v, v]
    return pl.pallas_call(
        _attn_kernel,
        grid=(b, ATT_HEADS // ATT_HPS, nt),
        in_specs=in_specs,
        out_specs=pl.BlockSpec((None, ATT_TT, ATT_HPS * HEAD_DIM), lambda bi, h, i: (bi, i, h)),
        out_shape=jax.ShapeDtypeStruct((b, t, ATT_GW), BF16),
        scratch_shapes=[pltpu.VMEM((ATT_HPS, ATT_GROUPS, ATT_TT, HEAD_DIM), F32),
                        pltpu.VMEM((ATT_HPS, ATT_GROUPS, ATT_TT, HEAD_DIM), F32)],
        compiler_params=pltpu.CompilerParams(
            dimension_semantics=("parallel", "parallel", "arbitrary"), vmem_limit_bytes=VMEM_LIMIT_BYTES_V7X),
        name="attn",
    )(*args)


def _merge_kernel(x_ref, oa_ref, ob_ref, sga_ref, sgb_ref, wa_ref, wb_ref, wo_ref, o_ref):
    ya = jnp.dot(oa_ref[...], wa_ref[...], preferred_element_type=F32)
    yb = jnp.dot(ob_ref[...], wb_ref[...], preferred_element_type=F32)
    merged = sga_ref[...].astype(F32) * ya + sgb_ref[...].astype(F32) * yb
    o_ref[...] = x_ref[...] + jnp.dot(merged.astype(BF16), wo_ref[...], preferred_element_type=F32)


def _merge(x2d, oa, ob, sga, sgb, wa, wb, wo, layer):
    n = x2d.shape[0]
    tile = lambda w: pl.BlockSpec((MERGE_TM, w), lambda i: (i, 0))
    return pl.pallas_call(
        _merge_kernel,
        grid=(n // MERGE_TM,),
        in_specs=[tile(D_MODEL), tile(HG_W), tile(ATT_GW), tile(D_MODEL), tile(D_MODEL),
                  _layer_block((HG_W, D_MODEL), layer), _layer_block((ATT_GW, D_MODEL), layer),
                  _layer_block((D_MODEL, D_MODEL), layer)],
        out_specs=tile(D_MODEL),
        out_shape=jax.ShapeDtypeStruct((n, D_MODEL), F32),
        compiler_params=pltpu.CompilerParams(
            dimension_semantics=("parallel",), vmem_limit_bytes=VMEM_LIMIT_BYTES_V7X),
        name="merge",
    )(x2d, oa, ob, sga, sgb, wa, wb, wo)


def kernel(x, ffn1_norm, ffn1_w_in, ffn1_w_out, mix_norm, w_in, hgrn_lb_logits, hgrn_out_norm, attn_q_norm,
           attn_k_norm, w_branch_a, w_branch_b, w_out, ffn2_norm, ffn2_w_in, ffn2_w_out):
    b, t, _ = x.shape
    assert x.shape[2] == D_MODEL and t % ATT_TT == 0 and t % HG_TB == 0
    assert COL_GA % PROJ_NC == 0 and COL_AQ % ATT_GW == 0
    tables = _rope_tables(t)
    ffn1_w_in, ffn1_w_out, w_in, w_branch_a, w_branch_b, w_out, ffn2_w_in, ffn2_w_out = (
        w.astype(BF16) for w in (ffn1_w_in, ffn1_w_out, w_in, w_branch_a, w_branch_b, w_out, ffn2_w_in, ffn2_w_out))
    x2d = x.reshape(b * t, D_MODEL)
    for l in range(DEPTH):
        x2d = _ffn(x2d, ffn1_norm[l], ffn1_w_in, ffn1_w_out, l)

        (qa, hf, va, gate, sga, sgb), qkv = _proj(
            x2d.reshape(b, t, D_MODEL), mix_norm[l], w_in, l, attn_q_norm[l], attn_k_norm[l], tables)
        ob = _attn([qkv[3 * g:3 * g + 3] for g in range(ATT_GROUPS)], b, t)
        x2d = _hgrn(l, qa, hf, va, gate, hgrn_lb_logits, hgrn_out_norm[l], x2d, ob.reshape(b * t, ATT_GW), sga, sgb,
                    w_branch_a, w_branch_b, w_out, b, t)

        x2d = _ffn(x2d, ffn2_norm[l], ffn2_w_in, ffn2_w_out, l)
    return x2d.reshape(b, t, D_MODEL)
```
